```python
import math
import jax, jax.numpy as jnp
from jax import lax
import numpy as np

D_MODEL = 2048
BATCH = 16
SEQ = 2048
DEPTH = 2

MIX_WIDTH = D_MODEL
HEAD_DIM = 128
ATT_WIDTH = MIX_WIDTH // 2
ATT_HEADS = ATT_WIDTH // HEAD_DIM
DILATED_PATTERN = ((128, 1), (512, 4), (2048, 16))
CONV_CHANNELS = MIX_WIDTH - ATT_WIDTH
CONV_K = 3
IN_WIDTH = 3 * ATT_WIDTH + 3 * CONV_CHANNELS
D_FF = 256 * ((8 * D_MODEL // 3 + 255) // 256)
S5_GROUP = 16
S5_STATE = 64
S5_GROUPS = D_MODEL // S5_GROUP
ROPE_THETA = 10000.0
RMS_EPS = 1e-6
NEG_INF = -1e30
N_AB = (DEPTH + 1) // 2
N_C = DEPTH // 2

kernel_name = 'hybrid_dilated_attn_shortconv_s5_macaron'


def rmsnorm(x, g):
    xf = x.astype(jnp.float32)
    y = xf * lax.rsqrt(jnp.mean(xf * xf, axis=-1, keepdims=True) + RMS_EPS)
    return (y * g.astype(jnp.float32)).astype(x.dtype)


def swiglu(x, w1, w3, w2):
    return (jax.nn.silu(x @ w1) * (x @ w3)) @ w2


def rope_tables(seq):
    half = HEAD_DIM // 2
    inv = ROPE_THETA ** (-jnp.arange(0, half, dtype=jnp.float32) * 2.0 / HEAD_DIM)
    ang = jnp.arange(seq, dtype=jnp.float32)[:, None] * inv[None, :]
    return jnp.cos(ang), jnp.sin(ang)


def apply_rope(t, cos, sin):
    half = HEAD_DIM // 2
    tf = t.astype(jnp.float32)
    t1, t2 = tf[..., :half], tf[..., half:]
    c, s = cos[:, None, :], sin[:, None, :]
    return jnp.concatenate([t1 * c - t2 * s, t2 * c + t1 * s], axis=-1)


def dilated_branch(q, k, v, window, dilation):
    bsz, seq, nh, dh = q.shape
    L = window // dilation
    n = seq // dilation
    nb = -(-n // L)
    pad = nb * L - n

    def to_blocks(t):
        t = t.reshape(bsz, n, dilation, nh, dh).transpose(0, 2, 1, 3, 4)
        t = jnp.pad(t, ((0, 0), (0, 0), (0, pad), (0, 0), (0, 0)))
        return t.reshape(bsz, dilation, nb, L, nh, dh)

    def with_prev(t):
        prev = jnp.pad(t, ((0, 0), (0, 0), (1, 0), (0, 0), (0, 0), (0, 0)))[:, :, :-1]
        return jnp.concatenate([prev, t], axis=3)

    qb = to_blocks(q)
    kw = with_prev(to_blocks(k))
    vw = with_prev(to_blocks(v))
    s = jnp.einsum('bgnqhd,bgnkhd->bghnqk', qb, kw)
    qi = jnp.arange(L)[:, None]
    kj = jnp.arange(2 * L)[None, :]
    dist = qi + L - kj
    band = (dist >= 0) & (dist <= L)
    kpos = jnp.arange(nb)[:, None, None] * L + kj[None] - L
    valid = band[None] & (kpos >= 0)
    s = jnp.where(valid, s, NEG_INF)
    m = jnp.max(s, axis=-1, keepdims=True)
    p = jnp.exp(s - m)
    denom = jnp.sum(p, axis=-1)
    o = jnp.einsum('bghnqk,bgnkhd->bgnqhd', p, vw)
    denom_t = denom.transpose(0, 1, 3, 4, 2)
    o = o / denom_t[..., None]
    lse = m[..., 0].transpose(0, 1, 3, 4, 2) + jnp.log(denom_t)

    def from_blocks(t):
        t = t.reshape((bsz, dilation, nb * L) + t.shape[4:])[:, :, :n]
        t = jnp.moveaxis(t, 1, 2)
        return t.reshape((bsz, seq) + t.shape[3:])

    return from_blocks(o), from_blocks(lse)


def mixer_attn_conv(h, w_in, conv_w, w_out, cos, sin):
    bsz, seq, _ = h.shape
    proj = h @ w_in
    a = ATT_WIDTH
    c = CONV_CHANNELS
    q, k, v, gate_b, gate_c, x_in = jnp.split(
        proj, [a, 2 * a, 3 * a, 3 * a + c, 3 * a + 2 * c], axis=-1)
    q = apply_rope(q.reshape(bsz, seq, ATT_HEADS, HEAD_DIM), cos, sin) * (HEAD_DIM ** -0.5)
    k = apply_rope(k.reshape(bsz, seq, ATT_HEADS, HEAD_DIM), cos, sin)
    v = v.reshape(bsz, seq, ATT_HEADS, HEAD_DIM).astype(jnp.float32)
    outs, lses = [], []
    for window, dilation in DILATED_PATTERN:
        o_i, l_i = dilated_branch(q, k, v, window, dilation)
        outs.append(o_i)
        lses.append(l_i)
    wts = jax.nn.softmax(jnp.stack(lses, axis=0), axis=0)
    att = jnp.einsum('nbsh,nbshd->bshd', wts, jnp.stack(outs, axis=0))
    att = att.reshape(bsz, seq, ATT_WIDTH).astype(h.dtype)
    u = gate_c * x_in
    conv = lax.conv_general_dilated(
        u, conv_w[:, None, :], window_strides=(1,), padding=[(CONV_K - 1, 0)],
        dimension_numbers=('NWC', 'WIO', 'NWC'), feature_group_count=CONV_CHANNELS)
    sc = gate_b * conv
    return jnp.concatenate([att, sc], axis=-1) @ w_out


def mixer_s5(u, lam_re, lam_im, log_dt, b_re, b_im, c_re, c_im, d_skip, w_a, w_b):
    bsz, seq, dm = u.shape
    uf = u.astype(jnp.float32).reshape(bsz, seq, S5_GROUPS, S5_GROUP)
    lr = lam_re.astype(jnp.float32)
    li = lam_im.astype(jnp.float32)
    dt = jnp.exp(log_dt.astype(jnp.float32))[:, None]
    mag = jnp.exp(lr * dt)
    ar = mag * jnp.cos(li * dt)
    ai = mag * jnp.sin(li * dt)
    den = lr * lr + li * li
    fr = ((ar - 1.0) * lr + ai * li) / den
    fi = (ai * lr - (ar - 1.0) * li) / den
    br = b_re.astype(jnp.float32)
    bi = b_im.astype(jnp.float32)
    bbar_re = fr[..., None] * br - fi[..., None] * bi
    bbar_im = fr[..., None] * bi + fi[..., None] * br
    bu_re = jnp.einsum('bsgc,gpc->bsgp', uf, bbar_re)
    bu_im = jnp.einsum('bsgc,gpc->bsgp', uf, bbar_im)
    a_re = jnp.broadcast_to(ar[None, None], (1, seq, S5_GROUPS, S5_STATE))
    a_im = jnp.broadcast_to(ai[None, None], (1, seq, S5_GROUPS, S5_STATE))

    def combine(e1, e2):
        a1r, a1i, b1r, b1i = e1
        a2r, a2i, b2r, b2i = e2
        return (a2r * a1r - a2i * a1i,
                a2r * a1i + a2i * a1r,
                a2r * b1r - a2i * b1i + b2r,
                a2r * b1i + a2i * b1r + b2i)

    _, _, s_re, s_im = lax.associative_scan(combine, (a_re, a_im, bu_re, bu_im), axis=1)
    y = (jnp.einsum('bsgp,gcp->bsgc', s_re, c_re.astype(jnp.float32))
         - jnp.einsum('bsgp,gcp->bsgc', s_im, c_im.astype(jnp.float32)))
    y = y + d_skip.astype(jnp.float32).reshape(S5_GROUPS, S5_GROUP) * uf
    y = jax.nn.gelu(y.reshape(bsz, seq, dm)).astype(u.dtype)
    return (y @ w_a) * jax.nn.sigmoid(y @ w_b)


def setup_inputs(seed: int = 0) -> dict:
    key = jax.random.key(seed)
    ks = jax.random.split(key, 24)
    f32 = jnp.float32
    nrm = lambda k, shape, scale: jax.random.normal(k, shape, f32) * scale
    x = jax.random.normal(ks[0], (BATCH, SEQ, D_MODEL), f32)
    ln_ffn_pre = 1.0 + nrm(ks[1], (DEPTH, D_MODEL), 0.01)
    ln_mix = 1.0 + nrm(ks[2], (DEPTH, D_MODEL), 0.01)
    ln_ffn_post = 1.0 + nrm(ks[3], (DEPTH, D_MODEL), 0.01)
    ln_final = 1.0 + nrm(ks[4], (D_MODEL,), 0.01)
    ffn_w1 = nrm(ks[5], (DEPTH, 2, D_MODEL, D_FF), D_MODEL ** -0.5)
    ffn_w3 = nrm(ks[6], (DEPTH, 2, D_MODEL, D_FF), D_MODEL ** -0.5)
    ffn_w2 = nrm(ks[7], (DEPTH, 2, D_FF, D_MODEL), D_FF ** -0.5)
    ab_w_in = nrm(ks[8], (N_AB, D_MODEL, IN_WIDTH), D_MODEL ** -0.5)
    ab_conv_w = nrm(ks[9], (N_AB, CONV_K, CONV_CHANNELS), CONV_K ** -0.5)
    ab_w_out = nrm(ks[10], (N_AB, MIX_WIDTH, D_MODEL), MIX_WIDTH ** -0.5)
    s5_lambda_re = -0.5 + nrm(ks[11], (N_C, S5_GROUPS, S5_STATE), 0.01)
    s5_lambda_im = (math.pi * jnp.arange(S5_STATE, dtype=f32))[None, None, :] + nrm(
        ks[12], (N_C, S5_GROUPS, S5_STATE), 0.01)
    s5_log_dt = jax.random.uniform(ks[13], (N_C, S5_GROUPS), f32,
                                   minval=math.log(1e-3), maxval=math.log(1e-1))
    s5_b_re = nrm(ks[14], (N_C, S5_GROUPS, S5_STATE, S5_GROUP), (2 * S5_GROUP) ** -0.5)
    s5_b_im = nrm(ks[15], (N_C, S5_GROUPS, S5_STATE, S5_GROUP), (2 * S5_GROUP) ** -0.5)
    s5_c_re = nrm(ks[16], (N_C, S5_GROUPS, S5_GROUP, S5_STATE), (2 * S5_STATE) ** -0.5)
    s5_c_im = nrm(ks[17], (N_C, S5_GROUPS, S5_GROUP, S5_STATE), (2 * S5_STATE) ** -0.5)
    s5_d = nrm(ks[18], (N_C, D_MODEL), 1.0)
    s5_glu_wa = nrm(ks[19], (N_C, D_MODEL, D_MODEL), D_MODEL ** -0.5)
    s5_glu_wb = nrm(ks[20], (N_C, D_MODEL, D_MODEL), D_MODEL ** -0.5)
    return {'x': x, 'ln_ffn_pre': ln_ffn_pre, 'ln_mix': ln_mix, 'ln_ffn_post': ln_ffn_post,
            'ln_final': ln_final, 'ffn_w1': ffn_w1, 'ffn_w3': ffn_w3, 'ffn_w2': ffn_w2,
            'ab_w_in': ab_w_in, 'ab_conv_w': ab_conv_w, 'ab_w_out': ab_w_out,
            's5_lambda_re': s5_lambda_re, 's5_lambda_im': s5_lambda_im, 's5_log_dt': s5_log_dt,
            's5_b_re': s5_b_re, 's5_b_im': s5_b_im, 's5_c_re': s5_c_re, 's5_c_im': s5_c_im,
            's5_d': s5_d, 's5_glu_wa': s5_glu_wa, 's5_glu_wb': s5_glu_wb}


def reference(x, ln_ffn_pre, ln_mix, ln_ffn_post, ln_final, ffn_w1, ffn_w3, ffn_w2,
              ab_w_in, ab_conv_w, ab_w_out, s5_lambda_re, s5_lambda_im, s5_log_dt,
              s5_b_re, s5_b_im, s5_c_re, s5_c_im, s5_d, s5_glu_wa, s5_glu_wb):
    cos, sin = rope_tables(x.shape[1])
    h = x
    for i in range(DEPTH):
        h = h + 0.5 * swiglu(rmsnorm(h, ln_ffn_pre[i]), ffn_w1[i, 0], ffn_w3[i, 0], ffn_w2[i, 0])
        u = rmsnorm(h, ln_mix[i])
        j = i // 2
        if i % 2 == 0:
            h = h + mixer_attn_conv(u, ab_w_in[j], ab_conv_w[j], ab_w_out[j], cos, sin)
        else:
            h = h + mixer_s5(u, s5_lambda_re[j], s5_lambda_im[j], s5_log_dt[j],
                             s5_b_re[j], s5_b_im[j], s5_c_re[j], s5_c_im[j],
                             s5_d[j], s5_glu_wa[j], s5_glu_wb[j])
        h = h + 0.5 * swiglu(rmsnorm(h, ln_ffn_post[i]), ffn_w1[i, 1], ffn_w3[i, 1], ffn_w2[i, 1])
    return rmsnorm(h, ln_final)
```

```python
import functools
import math

import numpy as np
import jax
import jax.numpy as jnp
from jax import lax
from jax.experimental import pallas as pl
from jax.experimental.pallas import tpu as pltpu

F32 = jnp.float32
BF16 = jnp.bfloat16

D_MODEL = 2048
HEAD_DIM = 128
ATT_WIDTH = 1024
ATT_HEADS = ATT_WIDTH // HEAD_DIM
CONV_CHANNELS = 1024
CONV_K = 3
DILATED_PATTERN = ((128, 1), (512, 4), (2048, 16))
S5_GROUP = 16
S5_STATE = 64
S5_GROUPS = D_MODEL // S5_GROUP
ROPE_THETA = 10000.0
RMS_EPS = 1e-6
NEG_INF = -1e30

LANES = 128
SUBLANES = 8
VMEM_CAP_BYTES = 60 * 2**20
SLAB_GROUPS = LANES // S5_GROUP
N_SLABS = S5_GROUPS // SLAB_GROUPS
SLAB_STATE = SLAB_GROUPS * S5_STATE
ATT_Q_TILE = 256


def _params(semantics, vmem_bytes):
    return pltpu.CompilerParams(dimension_semantics=semantics,
                                vmem_limit_bytes=min(int(vmem_bytes), VMEM_CAP_BYTES))


def _pick(n, pref):
    t = min(n, pref)
    while n % t:
        t //= 2
    return t


def _rms(x, g):
    ms = jnp.mean(x * x, axis=-1, keepdims=True)
    return x * lax.rsqrt(ms + RMS_EPS) * g


def _dot(a, b):
    return jnp.dot(a, b, preferred_element_type=F32)


def _row_spec(layout, tm, nsi, ngrid):
    if layout == "bs":
        idx = (lambda i, j: (i, 0)) if ngrid == 2 else (lambda i: (i, 0))
    else:
        idx = ((lambda i, j: (i % nsi, i // nsi)) if ngrid == 2
               else (lambda i: (i % nsi, i // nsi)))
    return idx


def _ffn_body(*refs, final_norm):
    if final_norm:
        h_ref, g_ref, w1_ref, w3_ref, w2_ref, gf_ref, o_ref, xn_ref = refs
    else:
        h_ref, g_ref, w1_ref, w3_ref, w2_ref, o_ref, xn_ref = refs
    j = pl.program_id(1)

    @pl.when(j == 0)
    def _():
        x = h_ref[...]
        xn_ref[...] = _rms(x, g_ref[...]).astype(BF16)
        o_ref[...] = x

    xn = xn_ref[...]
    a = _dot(xn, w1_ref[...])
    b = _dot(xn, w3_ref[...])
    act = (a * jax.nn.sigmoid(a)) * b * 0.5
    o_ref[...] += _dot(act.astype(BF16), w2_ref[...])

    if final_norm:
        @pl.when(j == pl.num_programs(1) - 1)
        def _():
            o_ref[...] = _rms(o_ref[...], gf_ref[...])


def _ffn(h, g, w1, w3, w2, layer, half, bsz, seq, in_layout, out_layout, g_final=None):
    d = D_MODEL
    f = w1.shape[-1]
    tm = _pick(seq, 1024)
    tf = _pick(f, 512)
    nsi = seq // tm
    shape = lambda lay: (bsz * seq, d) if lay == "bs" else (seq, bsz * d)
    h = h.reshape(shape(in_layout))
    in_specs = [
        pl.BlockSpec((tm, d), _row_spec(in_layout, tm, nsi, 2), pipeline_mode=pl.Buffered(1)),
        pl.BlockSpec((1, d), lambda i, j: (0, 0)),
        pl.BlockSpec((None, None, d, tf), lambda i, j: (layer, half, 0, j)),
        pl.BlockSpec((None, None, d, tf), lambda i, j: (layer, half, 0, j)),
        pl.BlockSpec((None, None, tf, d), lambda i, j: (layer, half, j, 0)),
    ]
    args = [h, g.reshape(1, d), w1, w3, w2]
    if g_final is not None:
        in_specs.append(pl.BlockSpec((1, d), lambda i, j: (0, 0)))
        args.append(g_final.reshape(1, d))
    vmem = (tm * d * 4 + 2 * tm * d * 4 + tm * d * 2 + 2 * 3 * d * tf * 2
            + 4 * tm * tf * 4 + 4 * 2**20)
    return pl.pallas_call(
        functools.partial(_ffn_body, final_norm=g_final is not None),
        grid=(bsz * nsi, f // tf),
        in_specs=in_specs,
        out_specs=pl.BlockSpec((tm, d), _row_spec(out_layout, tm, nsi, 2)),
        out_shape=jax.ShapeDtypeStruct(shape(out_layout), F32),
        scratch_shapes=[pltpu.VMEM((tm, d), BF16)],
        compiler_params=_params(("arbitrary", "arbitrary"), vmem),
        name="ffn",
    )(*args)


def _proj_body(h_ref, g_ref, wq_ref, wk_ref, wv_ref, wb_ref, wc_ref, wx_ref,
               cq_ref, sq_ref, ck_ref, sk_ref, cw_ref,
               q_ref, k_ref, v_ref, sc_ref, xn_ref, carry_ref, *, tiles_per_seq):
    i = pl.program_id(0)
    j = pl.program_id(1)
    tm, tn = q_ref.shape

    @pl.when(j == 0)
    def _():
        xn_ref[...] = _rms(h_ref[...], g_ref[...]).astype(BF16)

    xn = xn_ref[...]

    def rope(t, c, s):
        heads = []
        for hh in range(tn // HEAD_DIM):
            th = t[:, hh * HEAD_DIM:(hh + 1) * HEAD_DIM]
            heads.append(th * c + pltpu.roll(th, HEAD_DIM // 2, axis=1) * s)
        return jnp.concatenate(heads, axis=1)

    q_ref[...] = rope(_dot(xn, wq_ref[...]), cq_ref[...], sq_ref[...]).astype(BF16)
    k_ref[...] = rope(_dot(xn, wk_ref[...]), ck_ref[...], sk_ref[...]).astype(BF16)
    v_ref[...] = _dot(xn, wv_ref[...]).astype(BF16)

    u = _dot(xn, wc_ref[...]) * _dot(xn, wx_ref[...])

    @pl.when(i % tiles_per_seq == 0)
    def _():
        carry_ref[j] = jnp.zeros((SUBLANES, tn), F32)

    prev = carry_ref[j]
    row = lax.broadcasted_iota(jnp.int32, (tm, tn), 0)
    u1 = jnp.where(row == 0, prev[SUBLANES - 1:SUBLANES, :], pltpu.roll(u, 1, axis=0))
    u2 = jnp.where(row == 0, prev[SUBLANES - 2:SUBLANES - 1, :],
                   jnp.where(row == 1, prev[SUBLANES - 1:SUBLANES, :], pltpu.roll(u, 2, axis=0)))
    carry_ref[j] = u[tm - SUBLANES:tm, :]
    cw = cw_ref[...]
    conv = cw[0:1, :] * u2 + cw[1:2, :] * u1 + cw[2:3, :] * u
    sc_ref[...] = (_dot(xn, wb_ref[...]) * conv).astype(BF16)


def _proj(h, g, w_in, conv_w, layer, bsz, seq, rope):
    d = D_MODEL
    tm = _pick(seq, 1024)
    tn = 256
    nsi = seq // tm
    nj = ATT_WIDTH // tn
    wspec = lambda seg: pl.BlockSpec((None, d, tn), lambda i, j: (layer, 0, seg * nj + j))
    tspec = pl.BlockSpec((tm, HEAD_DIM), lambda i, j: (i % nsi, 0))
    ospec = pl.BlockSpec((tm, tn), lambda i, j: (i, j))
    oshape = jax.ShapeDtypeStruct((bsz * seq, ATT_WIDTH), BF16)
    vmem = (tm * d * 4 + tm * d * 2 + 2 * 6 * d * tn * 2 + 2 * 4 * tm * tn * 2
            + 2 * 4 * tm * HEAD_DIM * 4 + 10 * tm * tn * 4 + 4 * 2**20)
    return pl.pallas_call(
        functools.partial(_proj_body, tiles_per_seq=nsi),
        grid=(bsz * nsi, nj),
        in_specs=[pl.BlockSpec((tm, d), lambda i, j: (i, 0), pipeline_mode=pl.Buffered(1)),
                  pl.BlockSpec((1, d), lambda i, j: (0, 0))]
                 + [wspec(seg) for seg in range(6)]
                 + [tspec] * 4
                 + [pl.BlockSpec((None, CONV_K, tn), lambda i, j: (layer, 0, j))],
        out_specs=[ospec] * 4,
        out_shape=[oshape] * 4,
        scratch_shapes=[pltpu.VMEM((tm, d), BF16), pltpu.VMEM((nj, SUBLANES, tn), F32)],
        compiler_params=_params(("arbitrary", "arbitrary"), vmem),
        name="mix_proj",
    )(h, g.reshape(1, d), *([w_in] * 6), *rope, conv_w)


def _attn_bias(seq):
    r = np.arange(ATT_Q_TILE)[:, None]
    c = np.arange(seq)[None, :]
    delta = (seq - ATT_Q_TILE) + r - c
    count = np.zeros(delta.shape, np.int64)
    for window, dil in DILATED_PATTERN:
        count += (delta >= 0) & (delta % dil == 0) & (delta <= window)
    return np.where(count > 0, np.log(np.maximum(count, 1)), NEG_INF).astype(np.float32)


def _attn_body(q_ref, k_ref, v_ref, bias_ref, o_ref):
    seq = q_ref.shape[0]
    nq = seq // ATT_Q_TILE
    for i in range(nq):
        nk = (i + 1) * ATT_Q_TILE
        qi = q_ref[i * ATT_Q_TILE:nk, :]
        s = lax.dot_general(qi, k_ref[0:nk, :], (((1,), (1,)), ((), ())),
                            preferred_element_type=F32)
        s = s + bias_ref[:, seq - nk:seq]
        m = jnp.max(s, axis=-1, keepdims=True)
        p = jnp.exp(s - m)
        l = jnp.sum(p, axis=-1, keepdims=True)
        o = _dot(p.astype(BF16), v_ref[0:nk, :]) / l
        o_ref[i * ATT_Q_TILE:nk, :] = o.astype(o_ref.dtype)


def _attn(q, k, v, bsz, seq):
    bias = jnp.asarray(_attn_bias(seq))
    spec = pl.BlockSpec((seq, HEAD_DIM), lambda b, h: (b, h))
    vmem = 2 * 4 * seq * HEAD_DIM * 2 + 2 * ATT_Q_TILE * seq * 4 + 6 * ATT_Q_TILE * seq * 4 + 4 * 2**20
    return pl.pallas_call(
        _attn_body,
        grid=(bsz, ATT_HEADS),
        in_specs=[spec, spec, spec, pl.BlockSpec((ATT_Q_TILE, seq), lambda b, h: (0, 0))],
        out_specs=spec,
        out_shape=jax.ShapeDtypeStruct((bsz * seq, ATT_WIDTH), BF16),
        compiler_params=_params(("arbitrary", "arbitrary"), vmem),
        name="dilated_attn",
    )(q, k, v, bias)


def _outproj_body(h_ref, a_ref, s_ref, wa_ref, ws_ref, o_ref):
    o_ref[...] = h_ref[...] + _dot(a_ref[...], wa_ref[...]) + _dot(s_ref[...], ws_ref[...])


def _outproj(h, att, sc, w_out, layer, bsz, seq):
    d = D_MODEL
    tm = _pick(bsz * seq, 512)
    hspec = pl.BlockSpec((tm, d), lambda i: (i, 0))
    aspec = pl.BlockSpec((tm, ATT_WIDTH), lambda i: (i, 0))
    wspec = lambda seg: pl.BlockSpec((None, ATT_WIDTH, d), lambda i: (layer, seg, 0),
                                     pipeline_mode=pl.Buffered(1))
    vmem = 4 * tm * d * 4 + 4 * tm * ATT_WIDTH * 2 + 2 * ATT_WIDTH * d * 2 + 3 * tm * d * 4 + 4 * 2**20
    return pl.pallas_call(
        _outproj_body,
        grid=(bsz * seq // tm,),
        in_specs=[hspec, aspec, aspec, wspec(0), wspec(1)],
        out_specs=hspec,
        out_shape=jax.ShapeDtypeStruct((bsz * seq, d), F32),
        compiler_params=_params(("arbitrary",), vmem),
        name="mix_out",
    )(h, att, sc, w_out, w_out)


def _disc_body(lr_ref, li_ref, ldt_ref, bre_ref, bim_ref, cre_ref, cim_ref,
               ar_ref, ai_ref, bb_ref, cb_ref):
    lr = lr_ref[0]
    li = li_ref[0]
    dt = jnp.exp(ldt_ref[0])
    mag = jnp.exp(lr * dt)
    ar = mag * jnp.cos(li * dt)
    ai = mag * jnp.sin(li * dt)
    den = lr * lr + li * li
    fr = ((ar - 1.0) * lr + ai * li) / den
    fi = (ai * lr - (ar - 1.0) * li) / den
    bre = bre_ref[0]
    bim = bim_ref[0]
    ar_ref[0] = ar
    ai_ref[0] = ai
    bb_ref[0, :, 0:SLAB_STATE] = (fr * bre - fi * bim).astype(BF16)
    bb_ref[0, :, SLAB_STATE:2 * SLAB_STATE] = (fr * bim + fi * bre).astype(BF16)
    cb_ref[0, 0:SLAB_STATE, :] = cre_ref[0].astype(BF16)
    cb_ref[0, SLAB_STATE:2 * SLAB_STATE, :] = (-cim_ref[0]).astype(BF16)


def _blockdiag(t):
    _, g, a, b = t.shape
    eye = jnp.eye(g, dtype=t.dtype)
    return jnp.einsum("sgab,gh->sgahb", t, eye).reshape(N_SLABS, g * a, g * b)


def _s5_discretise(lam_re, lam_im, log_dt, b_re, b_im, c_re, c_im):
    vec = lambda t: t.reshape(N_SLABS, 1, SLAB_STATE)
    ldt = jnp.repeat(log_dt, S5_STATE)
    bblk = lambda t: _blockdiag(
        t.reshape(N_SLABS, SLAB_GROUPS, S5_STATE, S5_GROUP).transpose(0, 1, 3, 2))
    cblk = lambda t: _blockdiag(
        t.reshape(N_SLABS, SLAB_GROUPS, S5_GROUP, S5_STATE).transpose(0, 1, 3, 2))
    vspec = pl.BlockSpec((1, 1, SLAB_STATE), lambda s: (s, 0, 0))
    bspec = pl.BlockSpec((1, LANES, SLAB_STATE), lambda s: (s, 0, 0))
    cspec = pl.BlockSpec((1, SLAB_STATE, LANES), lambda s: (s, 0, 0))
    return pl.pallas_call(
        _disc_body,
        grid=(N_SLABS,),
        in_specs=[vspec, vspec, vspec, bspec, bspec, cspec, cspec],
        out_specs=[vspec, vspec,
                   pl.BlockSpec((1, LANES, 2 * SLAB_STATE), lambda s: (s, 0, 0)),
                   pl.BlockSpec((1, 2 * SLAB_STATE, LANES), lambda s: (s, 0, 0))],
        out_shape=[jax.ShapeDtypeStruct((N_SLABS, 1, SLAB_STATE), F32)] * 2
                  + [jax.ShapeDtypeStruct((N_SLABS, LANES, 2 * SLAB_STATE), BF16),
                     jax.ShapeDtypeStruct((N_SLABS, 2 * SLAB_STATE, LANES), BF16)],
        compiler_params=_params(("arbitrary",), 16 * 2**20),
        name="s5_discretise",
    )(vec(lam_re), vec(lam_im), vec(ldt), bblk(b_re), bblk(b_im), cblk(c_re), cblk(c_im))


def _gelu_tanh(x):
    return x * (0.5 * (1.0 + jnp.tanh(math.sqrt(2.0 / math.pi) * (x + 0.044715 * (x * x * x)))))


def _s5_body(h_ref, g_ref, ar_ref, ai_ref, bb_ref, cb_ref, d_ref, y_ref,
             du_ref, unb_ref, ys_ref, zs_ref, st_ref, *, steps, nb):
    @pl.when(pl.program_id(0) == 0)
    def _():
        st_ref[...] = jnp.zeros(st_ref.shape, F32)

    un = _rms(h_ref[...], g_ref[...])
    for s in range(N_SLABS):
        us = un[:, s * LANES:(s + 1) * LANES]
        du_ref[s] = us * d_ref[:, s * LANES:(s + 1) * LANES]
        unb_ref[s] = us.astype(BF16)

    def slab(s, _):
        zs_ref[...] = _dot(unb_ref[s], bb_ref[s])
        ar = jnp.broadcast_to(ar_ref[s], (nb, SLAB_STATE))
        ai = jnp.broadcast_to(ai_ref[s], (nb, SLAB_STATE))

        def step(t, z):
            z_re, z_im = z
            r = pl.ds(pl.multiple_of(t * nb, nb), nb)
            n_re = ar * z_re - ai * z_im + zs_ref[r, 0:SLAB_STATE]
            n_im = ar * z_im + ai * z_re + zs_ref[r, SLAB_STATE:2 * SLAB_STATE]
            zs_ref[r, 0:SLAB_STATE] = n_re
            zs_ref[r, SLAB_STATE:2 * SLAB_STATE] = n_im
            return n_re, n_im

        z0 = (st_ref[s, :, 0:SLAB_STATE], st_ref[s, :, SLAB_STATE:2 * SLAB_STATE])
        z_re, z_im = lax.fori_loop(0, steps, step, z0, unroll=4)
        st_ref[s, :, 0:SLAB_STATE] = z_re
        st_ref[s, :, SLAB_STATE:2 * SLAB_STATE] = z_im
        y = _dot(zs_ref[...].astype(BF16), cb_ref[s]) + du_ref[s]
        ys_ref[s] = _gelu_tanh(y).astype(BF16)
        return 0

    lax.fori_loop(0, N_SLABS, slab, 0)
    for s in range(N_SLABS):
        y_ref[:, s * LANES:(s + 1) * LANES] = ys_ref[s]


def _s5_scan(h, g, ar, ai, bb, cb, d_skip, bsz, seq):
    d = D_MODEL
    steps = _pick(seq, 32)
    rows = steps * bsz
    const = lambda shape: pl.BlockSpec(shape, lambda i: (0,) * len(shape))
    vmem = (2 * rows * d * 4 + 2 * rows * d * 2 + 2 * 2 * N_SLABS * LANES * 2 * SLAB_STATE * 2
            + rows * d * (4 + 2 + 2) + rows * 2 * SLAB_STATE * 4 + bsz * d * 8 * 4
            + 3 * rows * d * 4 + 4 * 2**20)
    return pl.pallas_call(
        functools.partial(_s5_body, steps=steps, nb=bsz),
        grid=(seq // steps,),
        in_specs=[pl.BlockSpec((rows, d), lambda i: (i, 0)),
                  const((1, d)),
                  const((N_SLABS, 1, SLAB_STATE)), const((N_SLABS, 1, SLAB_STATE)),
                  const((N_SLABS, LANES, 2 * SLAB_STATE)), const((N_SLABS, 2 * SLAB_STATE, LANES)),
                  const((1, d))],
        out_specs=pl.BlockSpec((rows, d), lambda i: (i, 0)),
        out_shape=jax.ShapeDtypeStruct((seq * bsz, d), BF16),
        scratch_shapes=[pltpu.VMEM((N_SLABS, rows, LANES), F32),
                        pltpu.VMEM((N_SLABS, rows, LANES), BF16),
                        pltpu.VMEM((N_SLABS, rows, LANES), BF16),
                        pltpu.VMEM((rows, 2 * SLAB_STATE), F32),
                        pltpu.VMEM((N_SLABS, bsz, 2 * SLAB_STATE), F32)],
        compiler_params=_params(("arbitrary",), vmem),
        name="s5_scan",
    )(h, g.reshape(1, d), ar, ai, bb, cb, d_skip.reshape(1, d))


def _glu_body(h_ref, y_ref, wa_ref, wb_ref, o_ref):
    y = y_ref[...]
    o_ref[...] = h_ref[...] + _dot(y, wa_ref[...]) * jax.nn.sigmoid(_dot(y, wb_ref[...]))


def _glu(h, y, wa, wb, layer):
    n, d = h.shape
    tm = _pick(n, 512)
    hspec = pl.BlockSpec((tm, d), lambda i: (i, 0))
    wspec = pl.BlockSpec((None, d, d), lambda i: (layer, 0, 0), pipeline_mode=pl.Buffered(1))
    vmem = 4 * tm * d * 4 + 2 * tm * d * 2 + 2 * d * d * 2 + 4 * tm * d * 4 + 4 * 2**20
    return pl.pallas_call(
        _glu_body,
        grid=(n // tm,),
        in_specs=[hspec, hspec, wspec, wspec],
        out_specs=hspec,
        out_shape=jax.ShapeDtypeStruct((n, d), F32),
        compiler_params=_params(("arbitrary",), vmem),
        name="s5_glu",
    )(h, y, wa, wb)


def _rope_tables(seq):
    half = HEAD_DIM // 2
    inv = ROPE_THETA ** (-jnp.arange(0, half, dtype=F32) * 2.0 / HEAD_DIM)
    ang = jnp.arange(seq, dtype=F32)[:, None] * inv[None, :]
    cos, sin = jnp.cos(ang), jnp.sin(ang)
    c2 = jnp.concatenate([cos, cos], axis=-1)
    s2 = jnp.concatenate([-sin, sin], axis=-1)
    scale = HEAD_DIM ** -0.5
    return c2 * scale, s2 * scale, c2, s2


def kernel(x, ln_ffn_pre, ln_mix, ln_ffn_post, ln_final, ffn_w1, ffn_w3, ffn_w2, ab_w_in, ab_conv_w, ab_w_out, s5_lambda_re, s5_lambda_im, s5_log_dt, s5_b_re, s5_b_im, s5_c_re, s5_c_im, s5_d, s5_glu_wa, s5_glu_wb):
    bsz, seq, d = x.shape
    depth = ln_mix.shape[0]
    assert d == D_MODEL and seq % ATT_Q_TILE == 0 and bsz % SUBLANES == 0
    w1, w3, w2 = ffn_w1.astype(BF16), ffn_w3.astype(BF16), ffn_w2.astype(BF16)
    w_in, w_out = ab_w_in.astype(BF16), ab_w_out.astype(BF16)
    wa, wb = s5_glu_wa.astype(BF16), s5_glu_wb.astype(BF16)
    rope = _rope_tables(seq)

    h = x.reshape(bsz * seq, d)
    layout = "bs"
    for i in range(depth):
        j = i // 2
        last = i == depth - 1
        mix_layout = "bs" if i % 2 == 0 else "sb"
        h = _ffn(h, ln_ffn_pre[i], w1, w3, w2, i, 0, bsz, seq, layout, mix_layout)
        layout = mix_layout
        if i % 2 == 0:
            q, k, v, sc = _proj(h, ln_mix[i], w_in, ab_conv_w, j, bsz, seq, rope)
            att = _attn(q, k, v, bsz, seq)
            h = _outproj(h, att, sc, w_out, j, bsz, seq)
        else:
            h = h.reshape(seq * bsz, d)
            ar, ai, bb, cb = _s5_discretise(s5_lambda_re[j], s5_lambda_im[j], s5_log_dt[j],
                                            s5_b_re[j], s5_b_im[j], s5_c_re[j], s5_c_im[j])
            y = _s5_scan(h, ln_mix[i], ar, ai, bb, cb, s5_d[j], bsz, seq)
            h = _glu(h, y, wa, wb, j)
        out_layout = "bs" if last else layout
        h = _ffn(h, ln_ffn_post[i], w1, w3, w2, i, 1, bsz, seq, layout, out_layout,
                 g_final=ln_final if last else None)
        layout = out_layout
    return h.reshape(bsz, seq, d)
```

```python
import functools
import math

import numpy as np
import jax
import jax.numpy as jnp
from jax import lax
from jax.experimental import pallas as pl
from jax.experimental.pallas import tpu as pltpu

F32 = jnp.float32
BF16 = jnp.bfloat16

D_MODEL = 2048
HEAD_DIM = 128
ATT_WIDTH = 1024
ATT_HEADS = ATT_WIDTH // HEAD_DIM
CONV_CHANNELS = 1024
CONV_K = 3
DILATED_PATTERN = ((128, 1), (512, 4), (2048, 16))
S5_GROUP = 16
S5_STATE = 64
S5_GROUPS = D_MODEL // S5_GROUP
ROPE_THETA = 10000.0
RMS_EPS = 1e-6
NEG_INF = -1e30

LANES = 128
SUBLANES = 8
VMEM_CAP_BYTES = 60 * 2**20
SLAB_GROUPS = LANES // S5_GROUP
N_SLABS = S5_GROUPS // SLAB_GROUPS
SLAB_STATE = SLAB_GROUPS * S5_STATE
ATT_Q_TILE = 256


def _params(semantics, vmem_bytes):
    return pltpu.CompilerParams(dimension_semantics=semantics,
                                vmem_limit_bytes=min(int(vmem_bytes), VMEM_CAP_BYTES))


def _pick(n, pref):
    t = min(n, pref)
    while n % t:
        t //= 2
    return t


def _rms(x, g):
    ms = jnp.mean(x * x, axis=-1, keepdims=True)
    return x * lax.rsqrt(ms + RMS_EPS) * g


def _dot(a, b):
    return jnp.dot(a, b, preferred_element_type=F32)


def _ffn_body(*refs, final_norm):
    if final_norm:
        h_ref, g_ref, w1_ref, w3_ref, w2_ref, gf_ref, o_ref, xn_ref = refs
    else:
        h_ref, g_ref, w1_ref, w3_ref, w2_ref, o_ref, xn_ref = refs
    j = pl.program_id(1)

    @pl.when(j == 0)
    def _():
        x = h_ref[...]
        xn_ref[...] = _rms(x, g_ref[...]).astype(BF16)
        o_ref[...] = x

    xn = xn_ref[...]
    a = _dot(xn, w1_ref[...])
    b = _dot(xn, w3_ref[...])
    act = (a * jax.nn.sigmoid(a)) * b * 0.5
    o_ref[...] += _dot(act.astype(BF16), w2_ref[...])

    if final_norm:
        @pl.when(j == pl.num_programs(1) - 1)
        def _():
            o_ref[...] = _rms(o_ref[...], gf_ref[...])


def _ffn(h, g, w1, w3, w2, layer, half, g_final=None, tm=1024, tf=512, single_buffer_h=True):
    n, d = h.shape
    f = w1.shape[-1]
    tm = _pick(n, tm)
    tf = _pick(f, tf)
    hmode = dict(pipeline_mode=pl.Buffered(1)) if single_buffer_h else {}
    in_specs = [
        pl.BlockSpec((tm, d), lambda i, j: (i, 0), **hmode),
        pl.BlockSpec((1, d), lambda i, j: (0, 0)),
        pl.BlockSpec((None, None, d, tf), lambda i, j: (layer, half, 0, j)),
        pl.BlockSpec((None, None, d, tf), lambda i, j: (layer, half, 0, j)),
        pl.BlockSpec((None, None, tf, d), lambda i, j: (layer, half, j, 0)),
    ]
    args = [h, g.reshape(1, d), w1, w3, w2]
    if g_final is not None:
        in_specs.append(pl.BlockSpec((1, d), lambda i, j: (0, 0)))
        args.append(g_final.reshape(1, d))
    vmem = ((1 if single_buffer_h else 2) * tm * d * 4 + 2 * tm * d * 4 + tm * d * 2
            + 2 * 3 * d * tf * 2 + 4 * tm * tf * 4 + 4 * 2**20)
    return pl.pallas_call(
        functools.partial(_ffn_body, final_norm=g_final is not None),
        grid=(n // tm, f // tf),
        in_specs=in_specs,
        out_specs=pl.BlockSpec((tm, d), lambda i, j: (i, 0)),
        out_shape=jax.ShapeDtypeStruct((n, d), F32),
        scratch_shapes=[pltpu.VMEM((tm, d), BF16)],
        compiler_params=_params(("arbitrary", "arbitrary"), vmem),
        name="ffn",
    )(*args)


def _proj_body(h_ref, g_ref, wq_ref, wk_ref, wv_ref, wb_ref, wc_ref, wx_ref,
               cq_ref, sq_ref, ck_ref, sk_ref, cw_ref,
               q_ref, k_ref, v_ref, sc_ref, xn_ref, carry_ref, *, tiles_per_seq):
    i = pl.program_id(0)
    j = pl.program_id(1)
    tm, tn = q_ref.shape

    @pl.when(j == 0)
    def _():
        xn_ref[...] = _rms(h_ref[...], g_ref[...]).astype(BF16)

    xn = xn_ref[...]

    def rope(t, c, s):
        heads = []
        for hh in range(tn // HEAD_DIM):
            th = t[:, hh * HEAD_DIM:(hh + 1) * HEAD_DIM]
            heads.append(th * c + pltpu.roll(th, HEAD_DIM // 2, axis=1) * s)
        return jnp.concatenate(heads, axis=1)

    q_ref[...] = rope(_dot(xn, wq_ref[...]), cq_ref[...], sq_ref[...]).astype(BF16)
    k_ref[...] = rope(_dot(xn, wk_ref[...]), ck_ref[...], sk_ref[...]).astype(BF16)
    v_ref[...] = _dot(xn, wv_ref[...]).astype(BF16)

    u = _dot(xn, wc_ref[...]) * _dot(xn, wx_ref[...])

    @pl.when(i % tiles_per_seq == 0)
    def _():
        carry_ref[j] = jnp.zeros((SUBLANES, tn), F32)

    prev = carry_ref[j]
    row = lax.broadcasted_iota(jnp.int32, (tm, tn), 0)
    u1 = jnp.where(row == 0, prev[SUBLANES - 1:SUBLANES, :], pltpu.roll(u, 1, axis=0))
    u2 = jnp.where(row == 0, prev[SUBLANES - 2:SUBLANES - 1, :],
                   jnp.where(row == 1, prev[SUBLANES - 1:SUBLANES, :], pltpu.roll(u, 2, axis=0)))
    carry_ref[j] = u[tm - SUBLANES:tm, :]
    cw = cw_ref[...]
    conv = cw[0:1, :] * u2 + cw[1:2, :] * u1 + cw[2:3, :] * u
    sc_ref[...] = (_dot(xn, wb_ref[...]) * conv).astype(BF16)


def _proj(h, g, w_in, conv_w, layer, bsz, seq, rope):
    d = D_MODEL
    tm = _pick(seq, 1024)
    tn = 256
    nsi = seq // tm
    nj = ATT_WIDTH // tn
    wspec = lambda seg: pl.BlockSpec((None, d, tn), lambda i, j: (layer, 0, seg * nj + j))
    tspec = pl.BlockSpec((tm, HEAD_DIM), lambda i, j: (i % nsi, 0))
    ospec = pl.BlockSpec((tm, tn), lambda i, j: (i, j))
    oshape = jax.ShapeDtypeStruct((bsz * seq, ATT_WIDTH), BF16)
    vmem = (tm * d * 4 + tm * d * 2 + 2 * 6 * d * tn * 2 + 2 * 4 * tm * tn * 2
            + 2 * 4 * tm * HEAD_DIM * 4 + 10 * tm * tn * 4 + 4 * 2**20)
    return pl.pallas_call(
        functools.partial(_proj_body, tiles_per_seq=nsi),
        grid=(bsz * nsi, nj),
        in_specs=[pl.BlockSpec((tm, d), lambda i, j: (i, 0), pipeline_mode=pl.Buffered(1)),
                  pl.BlockSpec((1, d), lambda i, j: (0, 0))]
                 + [wspec(seg) for seg in range(6)]
                 + [tspec] * 4
                 + [pl.BlockSpec((None, CONV_K, tn), lambda i, j: (layer, 0, j))],
        out_specs=[ospec] * 4,
        out_shape=[oshape] * 4,
        scratch_shapes=[pltpu.VMEM((tm, d), BF16), pltpu.VMEM((nj, SUBLANES, tn), F32)],
        compiler_params=_params(("arbitrary", "arbitrary"), vmem),
        name="mix_proj",
    )(h, g.reshape(1, d), *([w_in] * 6), *rope, conv_w)


def _attn_bias(seq):
    r = np.arange(ATT_Q_TILE)[:, None]
    c = np.arange(seq)[None, :]
    delta = (seq - ATT_Q_TILE) + r - c
    count = np.zeros(delta.shape, np.int64)
    for window, dil in DILATED_PATTERN:
        count += (delta >= 0) & (delta % dil == 0) & (delta <= window)
    return np.where(count > 0, np.log(np.maximum(count, 1)), NEG_INF).astype(np.float32)


def _attn_body(q_ref, k_ref, v_ref, bias_ref, o_ref):
    seq = q_ref.shape[0]
    nq = seq // ATT_Q_TILE
    for i in range(nq):
        nk = (i + 1) * ATT_Q_TILE
        qi = q_ref[i * ATT_Q_TILE:nk, :]
        s = lax.dot_general(qi, k_ref[0:nk, :], (((1,), (1,)), ((), ())),
                            preferred_element_type=F32)
        s = s + bias_ref[:, seq - nk:seq]
        m = jnp.max(s, axis=-1, keepdims=True)
        p = jnp.exp(s - m)
        l = jnp.sum(p, axis=-1, keepdims=True)
        o = _dot(p.astype(BF16), v_ref[0:nk, :]) / l
        o_ref[i * ATT_Q_TILE:nk, :] = o.astype(o_ref.dtype)


def _attn(q, k, v, bsz, seq):
    bias = jnp.asarray(_attn_bias(seq))
    spec = pl.BlockSpec((seq, HEAD_DIM), lambda b, h: (b, h))
    vmem = 2 * 4 * seq * HEAD_DIM * 2 + 2 * ATT_Q_TILE * seq * 4 + 6 * ATT_Q_TILE * seq * 4 + 4 * 2**20
    return pl.pallas_call(
        _attn_body,
        grid=(bsz, ATT_HEADS),
        in_specs=[spec, spec, spec, pl.BlockSpec((ATT_Q_TILE, seq), lambda b, h: (0, 0))],
        out_specs=spec,
        out_shape=jax.ShapeDtypeStruct((bsz * seq, ATT_WIDTH), BF16),
        compiler_params=_params(("arbitrary", "arbitrary"), vmem),
        name="dilated_attn",
    )(q, k, v, bias)


def _outproj_body(h_ref, a_ref, s_ref, wa_ref, ws_ref, o_ref):
    o_ref[...] = h_ref[...] + _dot(a_ref[...], wa_ref[...]) + _dot(s_ref[...], ws_ref[...])


def _outproj(h, att, sc, w_out, layer, bsz, seq):
    d = D_MODEL
    tm = _pick(bsz * seq, 512)
    hspec = pl.BlockSpec((tm, d), lambda i: (i, 0))
    aspec = pl.BlockSpec((tm, ATT_WIDTH), lambda i: (i, 0))
    wspec = lambda seg: pl.BlockSpec((None, ATT_WIDTH, d), lambda i: (layer, seg, 0),
                                     pipeline_mode=pl.Buffered(1))
    vmem = 4 * tm * d * 4 + 4 * tm * ATT_WIDTH * 2 + 2 * ATT_WIDTH * d * 2 + 3 * tm * d * 4 + 4 * 2**20
    return pl.pallas_call(
        _outproj_body,
        grid=(bsz * seq // tm,),
        in_specs=[hspec, aspec, aspec, wspec(0), wspec(1)],
        out_specs=hspec,
        out_shape=jax.ShapeDtypeStruct((bsz * seq, d), F32),
        compiler_params=_params(("arbitrary",), vmem),
        name="mix_out",
    )(h, att, sc, w_out, w_out)


def _disc_body(lr_ref, li_ref, ldt_ref, bre_ref, bim_ref, cre_ref, cim_ref,
               ar_ref, ai_ref, bb_ref, cb_ref):
    lr = lr_ref[0]
    li = li_ref[0]
    dt = jnp.exp(ldt_ref[0])
    mag = jnp.exp(lr * dt)
    ar = mag * jnp.cos(li * dt)
    ai = mag * jnp.sin(li * dt)
    den = lr * lr + li * li
    fr = ((ar - 1.0) * lr + ai * li) / den
    fi = (ai * lr - (ar - 1.0) * li) / den
    bre = bre_ref[0]
    bim = bim_ref[0]
    ar_ref[0] = ar
    ai_ref[0] = ai
    bb_ref[0, :, 0:SLAB_STATE] = (fr * bre - fi * bim).astype(BF16)
    bb_ref[0, :, SLAB_STATE:2 * SLAB_STATE] = (fr * bim + fi * bre).astype(BF16)
    cb_ref[0, 0:SLAB_STATE, :] = cre_ref[0].astype(BF16)
    cb_ref[0, SLAB_STATE:2 * SLAB_STATE, :] = (-cim_ref[0]).astype(BF16)


def _blockdiag(t):
    _, g, a, b = t.shape
    eye = jnp.eye(g, dtype=t.dtype)
    return jnp.einsum("sgab,gh->sgahb", t, eye).reshape(N_SLABS, g * a, g * b)


def _s5_discretise(lam_re, lam_im, log_dt, b_re, b_im, c_re, c_im):
    vec = lambda t: t.reshape(N_SLABS, 1, SLAB_STATE)
    ldt = jnp.repeat(log_dt, S5_STATE)
    bblk = lambda t: _blockdiag(
        t.reshape(N_SLABS, SLAB_GROUPS, S5_STATE, S5_GROUP).transpose(0, 1, 3, 2))
    cblk = lambda t: _blockdiag(
        t.reshape(N_SLABS, SLAB_GROUPS, S5_GROUP, S5_STATE).transpose(0, 1, 3, 2))
    vspec = pl.BlockSpec((1, 1, SLAB_STATE), lambda s: (s, 0, 0))
    bspec = pl.BlockSpec((1, LANES, SLAB_STATE), lambda s: (s, 0, 0))
    cspec = pl.BlockSpec((1, SLAB_STATE, LANES), lambda s: (s, 0, 0))
    return pl.pallas_call(
        _disc_body,
        grid=(N_SLABS,),
        in_specs=[vspec, vspec, vspec, bspec, bspec, cspec, cspec],
        out_specs=[vspec, vspec,
                   pl.BlockSpec((1, LANES, 2 * SLAB_STATE), lambda s: (s, 0, 0)),
                   pl.BlockSpec((1, 2 * SLAB_STATE, LANES), lambda s: (s, 0, 0))],
        out_shape=[jax.ShapeDtypeStruct((N_SLABS, 1, SLAB_STATE), F32)] * 2
                  + [jax.ShapeDtypeStruct((N_SLABS, LANES, 2 * SLAB_STATE), BF16),
                     jax.ShapeDtypeStruct((N_SLABS, 2 * SLAB_STATE, LANES), BF16)],
        compiler_params=_params(("arbitrary",), 16 * 2**20),
        name="s5_discretise",
    )(vec(lam_re), vec(lam_im), vec(ldt), bblk(b_re), bblk(b_im), cblk(c_re), cblk(c_im))


def _gelu_tanh(x):
    return x * (0.5 * (1.0 + jnp.tanh(math.sqrt(2.0 / math.pi) * (x + 0.044715 * (x * x * x)))))


def _s5_body(h_ref, g_ref, ar_ref, ai_ref, bb_ref, cb_ref, d_ref, y_ref,
             ut_ref, ys_ref, zs0_ref, zs1_ref, st_ref, *, steps, nb):
    @pl.when(pl.program_id(0) == 0)
    def _():
        st_ref[...] = jnp.zeros(st_ref.shape, F32)

    g = g_ref[...]
    for b in range(nb):
        un = _rms(h_ref[b], g)
        for s in range(N_SLABS):
            ut_ref[s, pl.ds(b, steps, stride=nb), :] = un[:, s * LANES:(s + 1) * LANES]

    def input_map(s, zs_ref):
        zs_ref[...] = _dot(ut_ref[s].astype(BF16), bb_ref[s])

    def recurrence(s, zs_ref):
        ar = jnp.broadcast_to(ar_ref[s], (nb, SLAB_STATE))
        ai = jnp.broadcast_to(ai_ref[s], (nb, SLAB_STATE))
        z_re = st_ref[s, :, 0:SLAB_STATE]
        z_im = st_ref[s, :, SLAB_STATE:2 * SLAB_STATE]
        for t in range(steps):
            r = slice(t * nb, (t + 1) * nb)
            z_re, z_im = (ar * z_re - ai * z_im + zs_ref[r, 0:SLAB_STATE],
                          ar * z_im + ai * z_re + zs_ref[r, SLAB_STATE:2 * SLAB_STATE])
            zs_ref[r, 0:SLAB_STATE] = z_re
            zs_ref[r, SLAB_STATE:2 * SLAB_STATE] = z_im
        st_ref[s, :, 0:SLAB_STATE] = z_re
        st_ref[s, :, SLAB_STATE:2 * SLAB_STATE] = z_im

    def output_map(s, zs_ref):
        y = _dot(zs_ref[...].astype(BF16), cb_ref[s]) + ut_ref[s] * d_ref[s]
        ys_ref[s] = _gelu_tanh(y)

    input_map(0, zs0_ref)
    input_map(1, zs1_ref)
    recurrence(0, zs0_ref)

    def pair(p, _):
        s = 2 * p
        output_map(s - 2, zs0_ref)
        input_map(s, zs0_ref)
        recurrence(s - 1, zs1_ref)
        output_map(s - 1, zs1_ref)
        input_map(s + 1, zs1_ref)
        recurrence(s, zs0_ref)
        return 0

    lax.fori_loop(1, N_SLABS // 2, pair, 0)
    output_map(N_SLABS - 2, zs0_ref)
    recurrence(N_SLABS - 1, zs1_ref)
    output_map(N_SLABS - 1, zs1_ref)

    for b in range(nb):
        for s in range(N_SLABS):
            y_ref[b, :, s * LANES:(s + 1) * LANES] = (
                ys_ref[s, pl.ds(b, steps, stride=nb), :].astype(BF16))


def _s5_scan(h, g, ar, ai, bb, cb, d_skip, bsz, seq):
    d = D_MODEL
    steps = _pick(seq, 32)
    rows = steps * bsz
    const = lambda shape: pl.BlockSpec(shape, lambda i: (0,) * len(shape),
                                       pipeline_mode=pl.Buffered(1))
    blk = pl.BlockSpec((bsz, steps, d), lambda i: (0, i, 0))
    vmem = (2 * rows * d * 4 + 2 * rows * d * 2 + 2 * N_SLABS * LANES * 2 * SLAB_STATE * 2
            + 2 * rows * d * 4 + 2 * rows * 2 * SLAB_STATE * 4 + bsz * d * 8 * 4
            + 2 * rows * d * 4 + 4 * 2**20)
    return pl.pallas_call(
        functools.partial(_s5_body, steps=steps, nb=bsz),
        grid=(seq // steps,),
        in_specs=[blk,
                  const((1, d)),
                  const((N_SLABS, 1, SLAB_STATE)), const((N_SLABS, 1, SLAB_STATE)),
                  const((N_SLABS, LANES, 2 * SLAB_STATE)), const((N_SLABS, 2 * SLAB_STATE, LANES)),
                  const((N_SLABS, 1, LANES))],
        out_specs=blk,
        out_shape=jax.ShapeDtypeStruct((bsz, seq, d), BF16),
        scratch_shapes=[pltpu.VMEM((N_SLABS, rows, LANES), F32),
                        pltpu.VMEM((N_SLABS, rows, LANES), F32),
                        pltpu.VMEM((rows, 2 * SLAB_STATE), F32),
                        pltpu.VMEM((rows, 2 * SLAB_STATE), F32),
                        pltpu.VMEM((N_SLABS, bsz, 2 * SLAB_STATE), F32)],
        compiler_params=_params(("arbitrary",), vmem),
        name="s5_scan",
    )(h, g.reshape(1, d), ar, ai, bb, cb, d_skip.reshape(N_SLABS, 1, LANES))


def _glu_body(h_ref, y_ref, wa_ref, wb_ref, o_ref):
    y = y_ref[...]
    o_ref[...] = h_ref[...] + _dot(y, wa_ref[...]) * jax.nn.sigmoid(_dot(y, wb_ref[...]))


def _glu(h, y, wa, wb, layer):
    n, d = h.shape
    tm = _pick(n, 512)
    hspec = pl.BlockSpec((tm, d), lambda i: (i, 0))
    wspec = pl.BlockSpec((None, d, d), lambda i: (layer, 0, 0), pipeline_mode=pl.Buffered(1))
    vmem = 4 * tm * d * 4 + 2 * tm * d * 2 + 2 * d * d * 2 + 4 * tm * d * 4 + 4 * 2**20
    return pl.pallas_call(
        _glu_body,
        grid=(n // tm,),
        in_specs=[hspec, hspec, wspec, wspec],
        out_specs=hspec,
        out_shape=jax.ShapeDtypeStruct((n, d), F32),
        compiler_params=_params(("arbitrary",), vmem),
        name="s5_glu",
    )(h, y, wa, wb)


def _rope_tables(seq):
    half = HEAD_DIM // 2
    inv = ROPE_THETA ** (-jnp.arange(0, half, dtype=F32) * 2.0 / HEAD_DIM)
    ang = jnp.arange(seq, dtype=F32)[:, None] * inv[None, :]
    cos, sin = jnp.cos(ang), jnp.sin(ang)
    c2 = jnp.concatenate([cos, cos], axis=-1)
    s2 = jnp.concatenate([-sin, sin], axis=-1)
    scale = HEAD_DIM ** -0.5
    return c2 * scale, s2 * scale, c2, s2


def kernel(x, ln_ffn_pre, ln_mix, ln_ffn_post, ln_final, ffn_w1, ffn_w3, ffn_w2, ab_w_in, ab_conv_w, ab_w_out, s5_lambda_re, s5_lambda_im, s5_log_dt, s5_b_re, s5_b_im, s5_c_re, s5_c_im, s5_d, s5_glu_wa, s5_glu_wb):
    bsz, seq, d = x.shape
    depth = ln_mix.shape[0]
    assert d == D_MODEL and seq % ATT_Q_TILE == 0 and bsz % SUBLANES == 0
    w1, w3, w2 = ffn_w1.astype(BF16), ffn_w3.astype(BF16), ffn_w2.astype(BF16)
    w_in, w_out = ab_w_in.astype(BF16), ab_w_out.astype(BF16)
    wa, wb = s5_glu_wa.astype(BF16), s5_glu_wb.astype(BF16)
    rope = _rope_tables(seq)

    h = x.reshape(bsz * seq, d)
    ffn_cfg = [dict(), dict(tm=512, tf=512, single_buffer_h=False),
               dict(tm=1024, tf=256, single_buffer_h=False), dict()]
    for i in range(depth):
        j = i // 2
        last = i == depth - 1
        h = _ffn(h, ln_ffn_pre[i], w1, w3, w2, i, 0, **ffn_cfg[2 * i])
        if i % 2 == 0:
            q, k, v, sc = _proj(h, ln_mix[i], w_in, ab_conv_w, j, bsz, seq, rope)
            att = _attn(q, k, v, bsz, seq)
            h = _outproj(h, att, sc, w_out, j, bsz, seq)
        else:
            ar, ai, bb, cb = _s5_discretise(s5_lambda_re[j], s5_lambda_im[j], s5_log_dt[j],
                                            s5_b_re[j], s5_b_im[j], s5_c_re[j], s5_c_im[j])
            y = _s5_scan(h.reshape(bsz, seq, d), ln_mix[i], ar, ai, bb, cb, s5_d[j], bsz, seq)
            h = _glu(h, y.reshape(bsz * seq, d), wa, wb, j)
        h = _ffn(h, ln_ffn_post[i], w1, w3, w2, i, 1, g_final=ln_final if last else None,
                 **ffn_cfg[2 * i + 1])
    return h.reshape(bsz, seq, d)
```

```python
import functools
import math

import numpy as np
import jax
import jax.numpy as jnp
from jax import lax
from jax.experimental import pallas as pl
from jax.experimental.pallas import tpu as pltpu

F32 = jnp.float32
BF16 = jnp.bfloat16

D_MODEL = 2048
HEAD_DIM = 128
ATT_WIDTH = 1024
ATT_HEADS = ATT_WIDTH // HEAD_DIM
CONV_CHANNELS = 1024
CONV_K = 3
DILATED_PATTERN = ((128, 1), (512, 4), (2048, 16))
S5_GROUP = 16
S5_STATE = 64
S5_GROUPS = D_MODEL // S5_GROUP
ROPE_THETA = 10000.0
RMS_EPS = 1e-6
NEG_INF = -1e30

LANES = 128
SUBLANES = 8
VMEM_CAP_BYTES = 60 * 2**20
SLAB_GROUPS = LANES // S5_GROUP
N_SLABS = S5_GROUPS // SLAB_GROUPS
SLAB_STATE = SLAB_GROUPS * S5_STATE
ATT_Q_TILE = 256


def _params(semantics, vmem_bytes):
    return pltpu.CompilerParams(dimension_semantics=semantics,
                                vmem_limit_bytes=min(int(vmem_bytes), VMEM_CAP_BYTES))


def _pick(n, pref):
    t = min(n, pref)
    while n % t:
        t //= 2
    return t


def _rms(x, g):
    ms = jnp.mean(x * x, axis=-1, keepdims=True)
    return x * lax.rsqrt(ms + RMS_EPS) * g


def _dot(a, b):
    return jnp.dot(a, b, preferred_element_type=F32)


def _ffn_body(*refs, final_norm, merged):
    h_ref, g_ref = refs[:2]
    w_refs = refs[2:4] if merged else refs[2:5]
    gf_ref = refs[-3] if final_norm else None
    o_ref, xn_ref = refs[-2:]
    j = pl.program_id(1)

    @pl.when(j == 0)
    def _():
        x = h_ref[...]
        xn_ref[...] = _rms(x, g_ref[...]).astype(BF16)
        o_ref[...] = x

    xn = xn_ref[...]
    if merged:
        w13_ref, w2_ref = w_refs
        ab = _dot(xn, w13_ref[...])
        tf = ab.shape[1] // 2
        a, b = ab[:, :tf], ab[:, tf:]
    else:
        w1_ref, w3_ref, w2_ref = w_refs
        a = _dot(xn, w1_ref[...])
        b = _dot(xn, w3_ref[...])
    act = (a * jax.nn.sigmoid(a)) * b * 0.5
    o_ref[...] += _dot(act.astype(BF16), w2_ref[...])

    if final_norm:
        @pl.when(j == pl.num_programs(1) - 1)
        def _():
            o_ref[...] = _rms(o_ref[...], gf_ref[...])


def _gate_weights(w1, w3, tf):
    l, k, d, f = w1.shape
    blocks = lambda w: w.astype(BF16).reshape(l, k, d, f // tf, 1, tf)
    return jnp.concatenate([blocks(w1), blocks(w3)], axis=4).reshape(l, k, d, 2 * f)


def _ffn(h, g, w13, w2, layer, half, g_final=None, tm=1024, tf=512, single_buffer_h=True):
    n, d = h.shape
    merged = not isinstance(w13, tuple)
    f = w2.shape[-2]
    tm = _pick(n, tm)
    tf = _pick(f, tf)
    hmode = dict(pipeline_mode=pl.Buffered(1)) if single_buffer_h else {}
    in_specs = [pl.BlockSpec((tm, d), lambda i, j: (i, 0), **hmode),
                pl.BlockSpec((1, d), lambda i, j: (0, 0))]
    if merged:
        in_specs.append(pl.BlockSpec((None, None, d, 2 * tf), lambda i, j: (layer, half, 0, j)))
        args = [h, g.reshape(1, d), w13, w2]
    else:
        in_specs += [pl.BlockSpec((None, None, d, tf), lambda i, j: (layer, half, 0, j))] * 2
        args = [h, g.reshape(1, d), *w13, w2]
    in_specs.append(pl.BlockSpec((None, None, tf, d), lambda i, j: (layer, half, j, 0)))
    if g_final is not None:
        in_specs.append(pl.BlockSpec((1, d), lambda i, j: (0, 0)))
        args.append(g_final.reshape(1, d))
    vmem = ((1 if single_buffer_h else 2) * tm * d * 4 + 2 * tm * d * 4 + tm * d * 2
            + 2 * 3 * d * tf * 2 + 4 * tm * tf * 4 + 4 * 2**20)
    return pl.pallas_call(
        functools.partial(_ffn_body, final_norm=g_final is not None, merged=merged),
        grid=(n // tm, f // tf),
        in_specs=in_specs,
        out_specs=pl.BlockSpec((tm, d), lambda i, j: (i, 0)),
        out_shape=jax.ShapeDtypeStruct((n, d), F32),
        scratch_shapes=[pltpu.VMEM((tm, d), BF16)],
        compiler_params=_params(("arbitrary", "arbitrary"), vmem),
        name="ffn",
    )(*args)


def _proj_body(h_ref, g_ref, w_ref, cq_ref, sq_ref, ck_ref, sk_ref, cw_ref,
               q_ref, k_ref, v_ref, sc_ref, xn_ref, carry_ref, *, tiles_per_seq):
    i = pl.program_id(0)
    j = pl.program_id(1)
    tm, tn = q_ref.shape

    @pl.when(j == 0)
    def _():
        xn_ref[...] = _rms(h_ref[...], g_ref[...]).astype(BF16)

    p = _dot(xn_ref[...], w_ref[...])
    seg = lambda n: p[:, n * tn:(n + 1) * tn]

    def rope(t, c, s):
        heads = []
        for hh in range(tn // HEAD_DIM):
            th = t[:, hh * HEAD_DIM:(hh + 1) * HEAD_DIM]
            heads.append(th * c + pltpu.roll(th, HEAD_DIM // 2, axis=1) * s)
        return jnp.concatenate(heads, axis=1)

    q_ref[...] = rope(seg(0), cq_ref[...], sq_ref[...]).astype(BF16)
    k_ref[...] = rope(seg(1), ck_ref[...], sk_ref[...]).astype(BF16)
    v_ref[...] = seg(2).astype(BF16)

    u = seg(4) * seg(5)

    @pl.when(i % tiles_per_seq == 0)
    def _():
        carry_ref[j] = jnp.zeros((SUBLANES, tn), F32)

    prev = carry_ref[j]
    row = lax.broadcasted_iota(jnp.int32, (tm, tn), 0)
    u1 = jnp.where(row == 0, prev[SUBLANES - 1:SUBLANES, :], pltpu.roll(u, 1, axis=0))
    u2 = jnp.where(row == 0, prev[SUBLANES - 2:SUBLANES - 1, :],
                   jnp.where(row == 1, prev[SUBLANES - 1:SUBLANES, :], pltpu.roll(u, 2, axis=0)))
    carry_ref[j] = u[tm - SUBLANES:tm, :]
    cw = cw_ref[...]
    conv = cw[0:1, :] * u2 + cw[1:2, :] * u1 + cw[2:3, :] * u
    sc_ref[...] = (seg(3) * conv).astype(BF16)


PROJ_TN = 256
PROJ_SEGMENTS = 6


def _proj_weights(w_in):
    n, d, _ = w_in.shape
    nj = ATT_WIDTH // PROJ_TN
    w = w_in.astype(BF16).reshape(n, d, PROJ_SEGMENTS, nj, PROJ_TN)
    return w.transpose(0, 1, 3, 2, 4).reshape(n, d, PROJ_SEGMENTS * ATT_WIDTH)


def _proj(h, g, w_in, conv_w, layer, bsz, seq, rope):
    d = D_MODEL
    tm = _pick(seq, 1024)
    tn = PROJ_TN
    nsi = seq // tm
    nj = ATT_WIDTH // tn
    tspec = pl.BlockSpec((tm, HEAD_DIM), lambda i, j: (i % nsi, 0))
    ospec = pl.BlockSpec((tm, tn), lambda i, j: (i, j))
    oshape = jax.ShapeDtypeStruct((bsz * seq, ATT_WIDTH), BF16)
    vmem = (2 * tm * d * 4 + tm * d * 2 + 2 * PROJ_SEGMENTS * d * tn * 2 + 2 * 4 * tm * tn * 2
            + 2 * 4 * tm * HEAD_DIM * 4 + 12 * tm * tn * 4 + 4 * 2**20)
    return pl.pallas_call(
        functools.partial(_proj_body, tiles_per_seq=nsi),
        grid=(bsz * nsi, nj),
        in_specs=[pl.BlockSpec((tm, d), lambda i, j: (i, 0)),
                  pl.BlockSpec((1, d), lambda i, j: (0, 0)),
                  pl.BlockSpec((None, d, PROJ_SEGMENTS * tn), lambda i, j: (layer, 0, j))]
                 + [tspec] * 4
                 + [pl.BlockSpec((None, CONV_K, tn), lambda i, j: (layer, 0, j))],
        out_specs=[ospec] * 4,
        out_shape=[oshape] * 4,
        scratch_shapes=[pltpu.VMEM((tm, d), BF16), pltpu.VMEM((nj, SUBLANES, tn), F32)],
        compiler_params=_params(("arbitrary", "arbitrary"), vmem),
        name="mix_proj",
    )(h, g.reshape(1, d), w_in, *rope, conv_w)


def _attn_bias(seq):
    r = np.arange(ATT_Q_TILE)[:, None]
    c = np.arange(seq)[None, :]
    delta = (seq - ATT_Q_TILE) + r - c
    count = np.zeros(delta.shape, np.int64)
    for window, dil in DILATED_PATTERN:
        count += (delta >= 0) & (delta % dil == 0) & (delta <= window)
    return np.where(count > 0, np.log(np.maximum(count, 1)), NEG_INF).astype(np.float32)


def _attn_body(q_ref, k_ref, v_ref, bias_ref, o_ref):
    seq = q_ref.shape[0]
    nq = seq // ATT_Q_TILE
    for i in range(nq):
        nk = (i + 1) * ATT_Q_TILE
        qi = q_ref[i * ATT_Q_TILE:nk, :]
        s = lax.dot_general(qi, k_ref[0:nk, :], (((1,), (1,)), ((), ())),
                            preferred_element_type=F32)
        s = s + bias_ref[:, seq - nk:seq]
        m = jnp.max(s, axis=-1, keepdims=True)
        p = jnp.exp(s - m)
        l = jnp.sum(p, axis=-1, keepdims=True)
        o = _dot(p.astype(BF16), v_ref[0:nk, :]) / l
        o_ref[i * ATT_Q_TILE:nk, :] = o.astype(o_ref.dtype)


def _attn(q, k, v, bsz, seq):
    bias = jnp.asarray(_attn_bias(seq))
    spec = pl.BlockSpec((seq, HEAD_DIM), lambda b, h: (b, h))
    vmem = 2 * 4 * seq * HEAD_DIM * 2 + 2 * ATT_Q_TILE * seq * 4 + 6 * ATT_Q_TILE * seq * 4 + 4 * 2**20
    return pl.pallas_call(
        _attn_body,
        grid=(bsz, ATT_HEADS),
        in_specs=[spec, spec, spec, pl.BlockSpec((ATT_Q_TILE, seq), lambda b, h: (0, 0))],
        out_specs=spec,
        out_shape=jax.ShapeDtypeStruct((bsz * seq, ATT_WIDTH), BF16),
        compiler_params=_params(("arbitrary", "arbitrary"), vmem),
        name="dilated_attn",
    )(q, k, v, bias)


def _outproj_body(h_ref, a_ref, s_ref, wa_ref, ws_ref, o_ref):
    o_ref[...] = h_ref[...] + _dot(a_ref[...], wa_ref[...]) + _dot(s_ref[...], ws_ref[...])


def _outproj(h, att, sc, w_out, layer, bsz, seq):
    d = D_MODEL
    tm = _pick(bsz * seq, 512)
    hspec = pl.BlockSpec((tm, d), lambda i: (i, 0))
    aspec = pl.BlockSpec((tm, ATT_WIDTH), lambda i: (i, 0))
    wspec = lambda seg: pl.BlockSpec((None, ATT_WIDTH, d), lambda i: (layer, seg, 0),
                                     pipeline_mode=pl.Buffered(1))
    vmem = 4 * tm * d * 4 + 4 * tm * ATT_WIDTH * 2 + 2 * ATT_WIDTH * d * 2 + 3 * tm * d * 4 + 4 * 2**20
    return pl.pallas_call(
        _outproj_body,
        grid=(bsz * seq // tm,),
        in_specs=[hspec, aspec, aspec, wspec(0), wspec(1)],
        out_specs=hspec,
        out_shape=jax.ShapeDtypeStruct((bsz * seq, d), F32),
        compiler_params=_params(("arbitrary",), vmem),
        name="mix_out",
    )(h, att, sc, w_out, w_out)


def _disc_body(lr_ref, li_ref, ldt_ref, bre_ref, bim_ref, cre_ref, cim_ref,
               ar_ref, ai_ref, bb_ref, cb_ref):
    lr = lr_ref[0]
    li = li_ref[0]
    dt = jnp.exp(ldt_ref[0])
    mag = jnp.exp(lr * dt)
    ar = mag * jnp.cos(li * dt)
    ai = mag * jnp.sin(li * dt)
    den = lr * lr + li * li
    fr = ((ar - 1.0) * lr + ai * li) / den
    fi = (ai * lr - (ar - 1.0) * li) / den
    bre = bre_ref[0]
    bim = bim_ref[0]
    ar_ref[0] = ar
    ai_ref[0] = ai
    bb_ref[0, :, 0:SLAB_STATE] = (fr * bre - fi * bim).astype(BF16)
    bb_ref[0, :, SLAB_STATE:2 * SLAB_STATE] = (fr * bim + fi * bre).astype(BF16)
    cb_ref[0, 0:SLAB_STATE, :] = cre_ref[0].astype(BF16)
    cb_ref[0, SLAB_STATE:2 * SLAB_STATE, :] = (-cim_ref[0]).astype(BF16)


def _blockdiag(t):
    _, g, a, b = t.shape
    eye = jnp.eye(g, dtype=t.dtype)
    return jnp.einsum("sgab,gh->sgahb", t, eye).reshape(N_SLABS, g * a, g * b)


def _s5_discretise(lam_re, lam_im, log_dt, b_re, b_im, c_re, c_im):
    vec = lambda t: t.reshape(N_SLABS, 1, SLAB_STATE)
    ldt = jnp.repeat(log_dt, S5_STATE)
    bblk = lambda t: _blockdiag(
        t.reshape(N_SLABS, SLAB_GROUPS, S5_STATE, S5_GROUP).transpose(0, 1, 3, 2))
    cblk = lambda t: _blockdiag(
        t.reshape(N_SLABS, SLAB_GROUPS, S5_GROUP, S5_STATE).transpose(0, 1, 3, 2))
    vspec = pl.BlockSpec((1, 1, SLAB_STATE), lambda s: (s, 0, 0))
    bspec = pl.BlockSpec((1, LANES, SLAB_STATE), lambda s: (s, 0, 0))
    cspec = pl.BlockSpec((1, SLAB_STATE, LANES), lambda s: (s, 0, 0))
    return pl.pallas_call(
        _disc_body,
        grid=(N_SLABS,),
        in_specs=[vspec, vspec, vspec, bspec, bspec, cspec, cspec],
        out_specs=[vspec, vspec,
                   pl.BlockSpec((1, LANES, 2 * SLAB_STATE), lambda s: (s, 0, 0)),
                   pl.BlockSpec((1, 2 * SLAB_STATE, LANES), lambda s: (s, 0, 0))],
        out_shape=[jax.ShapeDtypeStruct((N_SLABS, 1, SLAB_STATE), F32)] * 2
                  + [jax.ShapeDtypeStruct((N_SLABS, LANES, 2 * SLAB_STATE), BF16),
                     jax.ShapeDtypeStruct((N_SLABS, 2 * SLAB_STATE, LANES), BF16)],
        compiler_params=_params(("arbitrary",), 16 * 2**20),
        name="s5_discretise",
    )(vec(lam_re), vec(lam_im), vec(ldt), bblk(b_re), bblk(b_im), cblk(c_re), cblk(c_im))


def _gelu_tanh(x):
    return x * (0.5 * (1.0 + jnp.tanh(math.sqrt(2.0 / math.pi) * (x + 0.044715 * (x * x * x)))))


def _s5_body(h_ref, g_ref, ar_ref, ai_ref, bb_ref, cb_ref, d_ref, y_ref,
             ut_ref, ys_ref, zs0_ref, zs1_ref, st_ref, *, steps, nb):
    @pl.when(pl.program_id(0) == 0)
    def _():
        st_ref[...] = jnp.zeros(st_ref.shape, F32)

    g = g_ref[...]
    for b in range(nb):
        un = _rms(h_ref[b], g)
        for s in range(N_SLABS):
            ut_ref[s, pl.ds(b, steps, stride=nb), :] = un[:, s * LANES:(s + 1) * LANES]

    def input_map(s, zs_ref):
        zs_ref[...] = _dot(ut_ref[s].astype(BF16), bb_ref[s])

    def recurrence(s, zs_ref):
        ar = jnp.broadcast_to(ar_ref[s], (nb, SLAB_STATE))
        ai = jnp.broadcast_to(ai_ref[s], (nb, SLAB_STATE))
        z_re = st_ref[s, :, 0:SLAB_STATE]
        z_im = st_ref[s, :, SLAB_STATE:2 * SLAB_STATE]
        for t in range(steps):
            r = slice(t * nb, (t + 1) * nb)
            z_re, z_im = (ar * z_re - ai * z_im + zs_ref[r, 0:SLAB_STATE],
                          ar * z_im + ai * z_re + zs_ref[r, SLAB_STATE:2 * SLAB_STATE])
            zs_ref[r, 0:SLAB_STATE] = z_re
            zs_ref[r, SLAB_STATE:2 * SLAB_STATE] = z_im
        st_ref[s, :, 0:SLAB_STATE] = z_re
        st_ref[s, :, SLAB_STATE:2 * SLAB_STATE] = z_im

    def output_map(s, zs_ref):
        y = _dot(zs_ref[...].astype(BF16), cb_ref[s]) + ut_ref[s] * d_ref[s]
        ys_ref[s] = _gelu_tanh(y)

    input_map(0, zs0_ref)
    input_map(1, zs1_ref)
    recurrence(0, zs0_ref)

    def pair(p, _):
        s = 2 * p
        output_map(s - 2, zs0_ref)
        input_map(s, zs0_ref)
        recurrence(s - 1, zs1_ref)
        output_map(s - 1, zs1_ref)
        input_map(s + 1, zs1_ref)
        recurrence(s, zs0_ref)
        return 0

    lax.fori_loop(1, N_SLABS // 2, pair, 0)
    output_map(N_SLABS - 2, zs0_ref)
    recurrence(N_SLABS - 1, zs1_ref)
    output_map(N_SLABS - 1, zs1_ref)

    for b in range(nb):
        for s in range(N_SLABS):
            y_ref[b, :, s * LANES:(s + 1) * LANES] = (
                ys_ref[s, pl.ds(b, steps, stride=nb), :].astype(BF16))


def _s5_scan(h, g, ar, ai, bb, cb, d_skip, bsz, seq):
    d = D_MODEL
    steps = _pick(seq, 32)
    rows = steps * bsz
    const = lambda shape: pl.BlockSpec(shape, lambda i: (0,) * len(shape),
                                       pipeline_mode=pl.Buffered(1))
    blk = pl.BlockSpec((bsz, steps, d), lambda i: (0, i, 0))
    vmem = (2 * rows * d * 4 + 2 * rows * d * 2 + 2 * N_SLABS * LANES * 2 * SLAB_STATE * 2
            + 2 * rows * d * 4 + 2 * rows * 2 * SLAB_STATE * 4 + bsz * d * 8 * 4
            + 2 * rows * d * 4 + 4 * 2**20)
    return pl.pallas_call(
        functools.partial(_s5_body, steps=steps, nb=bsz),
        grid=(seq // steps,),
        in_specs=[blk,
                  const((1, d)),
                  const((N_SLABS, 1, SLAB_STATE)), const((N_SLABS, 1, SLAB_STATE)),
                  const((N_SLABS, LANES, 2 * SLAB_STATE)), const((N_SLABS, 2 * SLAB_STATE, LANES)),
                  const((N_SLABS, 1, LANES))],
        out_specs=blk,
        out_shape=jax.ShapeDtypeStruct((bsz, seq, d), BF16),
        scratch_shapes=[pltpu.VMEM((N_SLABS, rows, LANES), F32),
                        pltpu.VMEM((N_SLABS, rows, LANES), F32),
                        pltpu.VMEM((rows, 2 * SLAB_STATE), F32),
                        pltpu.VMEM((rows, 2 * SLAB_STATE), F32),
                        pltpu.VMEM((N_SLABS, bsz, 2 * SLAB_STATE), F32)],
        compiler_params=_params(("arbitrary",), vmem),
        name="s5_scan",
    )(h, g.reshape(1, d), ar, ai, bb, cb, d_skip.reshape(N_SLABS, 1, LANES))


def _glu_body(h_ref, y_ref, wa_ref, wb_ref, o_ref):
    y = y_ref[...]
    o_ref[...] = h_ref[...] + _dot(y, wa_ref[...]) * jax.nn.sigmoid(_dot(y, wb_ref[...]))


def _glu(h, y, wa, wb, layer):
    n, d = h.shape
    tm = _pick(n, 512)
    hspec = pl.BlockSpec((tm, d), lambda i: (i, 0))
    wspec = pl.BlockSpec((None, d, d), lambda i: (layer, 0, 0), pipeline_mode=pl.Buffered(1))
    vmem = 4 * tm * d * 4 + 2 * tm * d * 2 + 2 * d * d * 2 + 4 * tm * d * 4 + 4 * 2**20
    return pl.pallas_call(
        _glu_body,
        grid=(n // tm,),
        in_specs=[hspec, hspec, wspec, wspec],
        out_specs=hspec,
        out_shape=jax.ShapeDtypeStruct((n, d), F32),
        compiler_params=_params(("arbitrary",), vmem),
        name="s5_glu",
    )(h, y, wa, wb)


def _rope_tables(seq):
    half = HEAD_DIM // 2
    inv = ROPE_THETA ** (-jnp.arange(0, half, dtype=F32) * 2.0 / HEAD_DIM)
    ang = jnp.arange(seq, dtype=F32)[:, None] * inv[None, :]
    cos, sin = jnp.cos(ang), jnp.sin(ang)
    c2 = jnp.concatenate([cos, cos], axis=-1)
    s2 = jnp.concatenate([-sin, sin], axis=-1)
    scale = HEAD_DIM ** -0.5
    return c2 * scale, s2 * scale, c2, s2


def kernel(x, ln_ffn_pre, ln_mix, ln_ffn_post, ln_final, ffn_w1, ffn_w3, ffn_w2, ab_w_in, ab_conv_w, ab_w_out, s5_lambda_re, s5_lambda_im, s5_log_dt, s5_b_re, s5_b_im, s5_c_re, s5_c_im, s5_d, s5_glu_wa, s5_glu_wb):
    bsz, seq, d = x.shape
    depth = ln_mix.shape[0]
    assert d == D_MODEL and seq % ATT_Q_TILE == 0 and bsz % SUBLANES == 0
    w1, w3, w2 = ffn_w1.astype(BF16), ffn_w3.astype(BF16), ffn_w2.astype(BF16)
    w_in, w_out = _proj_weights(ab_w_in), ab_w_out.astype(BF16)
    wa, wb = s5_glu_wa.astype(BF16), s5_glu_wb.astype(BF16)
    rope = _rope_tables(seq)

    h = x.reshape(bsz * seq, d)
    dbl = dict(tm=1024, single_buffer_h=False)
    ffn_cfg = [((w1, w3), dict(tf=256, **dbl)),
               (_gate_weights(ffn_w1, ffn_w3, 256), dict(tf=256, **dbl)),
               (_gate_weights(ffn_w1, ffn_w3, 512), dict(tf=512, **dbl)),
               ((w1, w3), dict(tf=256, **dbl))]
    for i in range(depth):
        j = i // 2
        last = i == depth - 1
        w13, cfg = ffn_cfg[2 * i]
        h = _ffn(h, ln_ffn_pre[i], w13, w2, i, 0, **cfg)
        if i % 2 == 0:
            q, k, v, sc = _proj(h, ln_mix[i], w_in, ab_conv_w, j, bsz, seq, rope)
            att = _attn(q, k, v, bsz, seq)
            h = _outproj(h, att, sc, w_out, j, bsz, seq)
        else:
            ar, ai, bb, cb = _s5_discretise(s5_lambda_re[j], s5_lambda_im[j], s5_log_dt[j],
                                            s5_b_re[j], s5_b_im[j], s5_c_re[j], s5_c_im[j])
            y = _s5_scan(h.reshape(bsz, seq, d), ln_mix[i], ar, ai, bb, cb, s5_d[j], bsz, seq)
            h = _glu(h, y.reshape(bsz * seq, d), wa, wb, j)
        w13, cfg = ffn_cfg[2 * i + 1]
        h = _ffn(h, ln_ffn_post[i], w13, w2, i, 1, g_final=ln_final if last else None, **cfg)
    return h.reshape(bsz, seq, d)
```

```python
import functools
import math

import numpy as np
import jax
import jax.numpy as jnp
from jax import lax
from jax.experimental import pallas as pl
from jax.experimental.pallas import tpu as pltpu

F32 = jnp.float32
BF16 = jnp.bfloat16

D_MODEL = 2048
HEAD_DIM = 128
ATT_WIDTH = 1024
ATT_HEADS = ATT_WIDTH // HEAD_DIM
CONV_CHANNELS = 1024
CONV_K = 3
DILATED_PATTERN = ((128, 1), (512, 4), (2048, 16))
S5_GROUP = 16
S5_STATE = 64
S5_GROUPS = D_MODEL // S5_GROUP
ROPE_THETA = 10000.0
RMS_EPS = 1e-6
NEG_INF = -1e30

LANES = 128
SUBLANES = 8
VMEM_CAP_BYTES = 60 * 2**20
SLAB_GROUPS = LANES // S5_GROUP
N_SLABS = S5_GROUPS // SLAB_GROUPS
SLAB_STATE = SLAB_GROUPS * S5_STATE
ATT_Q_TILE = 512


def _params(semantics, vmem_bytes):
    return pltpu.CompilerParams(dimension_semantics=semantics,
                                vmem_limit_bytes=min(int(vmem_bytes), VMEM_CAP_BYTES))


def _pick(n, pref):
    t = min(n, pref)
    while n % t:
        t //= 2
    return t


def _rms(x, g):
    ms = jnp.mean(x * x, axis=-1, keepdims=True)
    return x * lax.rsqrt(ms + RMS_EPS) * g


def _dot(a, b):
    return jnp.dot(a, b, preferred_element_type=F32)


def _ffn_body(*refs, final_norm):
    if final_norm:
        h_ref, g_ref, w1_ref, w3_ref, w2_ref, gf_ref, o_ref, xn_ref = refs
    else:
        h_ref, g_ref, w1_ref, w3_ref, w2_ref, o_ref, xn_ref = refs
    j = pl.program_id(1)

    @pl.when(j == 0)
    def _():
        x = h_ref[...]
        xn_ref[...] = _rms(x, g_ref[...]).astype(BF16)
        o_ref[...] = x

    xn = xn_ref[...]
    a = _dot(xn, w1_ref[...])
    b = _dot(xn, w3_ref[...])
    act = (a * jax.nn.sigmoid(a)) * b * 0.5
    o_ref[...] += _dot(act.astype(BF16), w2_ref[...])

    if final_norm:
        @pl.when(j == pl.num_programs(1) - 1)
        def _():
            o_ref[...] = _rms(o_ref[...], gf_ref[...])


FFN_TM = 1024
FFN_TF = 512


def _ffn(h, g, w1, w3, w2, layer, half, g_final=None):
    n, d = h.shape
    f = w1.shape[-1]
    tm = _pick(n, FFN_TM)
    tf = _pick(f, FFN_TF)
    in_specs = [
        pl.BlockSpec((tm, d), lambda i, j: (i, 0)),
        pl.BlockSpec((1, d), lambda i, j: (0, 0)),
        pl.BlockSpec((None, None, d, tf), lambda i, j: (layer, half, 0, j)),
        pl.BlockSpec((None, None, d, tf), lambda i, j: (layer, half, 0, j)),
        pl.BlockSpec((None, None, tf, d), lambda i, j: (layer, half, j, 0)),
    ]
    args = [h, g.reshape(1, d), w1, w3, w2]
    if g_final is not None:
        in_specs.append(pl.BlockSpec((1, d), lambda i, j: (0, 0)))
        args.append(g_final.reshape(1, d))
    vmem = 4 * tm * d * 4 + tm * d * 2 + 2 * 3 * d * tf * 2 + 4 * tm * tf * 4 + 4 * 2**20
    return pl.pallas_call(
        functools.partial(_ffn_body, final_norm=g_final is not None),
        grid=(n // tm, f // tf),
        in_specs=in_specs,
        out_specs=pl.BlockSpec((tm, d), lambda i, j: (i, 0)),
        out_shape=jax.ShapeDtypeStruct((n, d), F32),
        scratch_shapes=[pltpu.VMEM((tm, d), BF16)],
        compiler_params=_params(("arbitrary", "arbitrary"), vmem),
        name="ffn",
    )(*args)


def _proj_body(h_ref, g_ref, w_ref, cq_ref, sq_ref, ck_ref, sk_ref, cw_ref,
               qt_ref, k_ref, vt_ref, sc_ref, xn_ref, carry_ref, *, tiles_per_seq):
    i = pl.program_id(0)
    j = pl.program_id(1)
    tm, tn = k_ref.shape

    @pl.when(j == 0)
    def _():
        xn_ref[...] = _rms(h_ref[...], g_ref[...]).astype(BF16)

    p = _dot(xn_ref[...], w_ref[...])
    seg = lambda n: p[:, n * tn:(n + 1) * tn]

    def rope(t, c, s):
        heads = []
        for hh in range(tn // HEAD_DIM):
            th = t[:, hh * HEAD_DIM:(hh + 1) * HEAD_DIM]
            heads.append(th * c + pltpu.roll(th, HEAD_DIM // 2, axis=1) * s)
        return jnp.concatenate(heads, axis=1)

    qt_ref[...] = rope(seg(0), cq_ref[...], sq_ref[...]).T.astype(BF16)
    k_ref[...] = rope(seg(1), ck_ref[...], sk_ref[...]).astype(BF16)
    vt_ref[...] = seg(2).T.astype(BF16)

    u = seg(4) * seg(5)

    @pl.when(i % tiles_per_seq == 0)
    def _():
        carry_ref[j] = jnp.zeros((SUBLANES, tn), F32)

    prev = carry_ref[j]
    row = lax.broadcasted_iota(jnp.int32, (tm, tn), 0)
    u1 = jnp.where(row == 0, prev[SUBLANES - 1:SUBLANES, :], pltpu.roll(u, 1, axis=0))
    u2 = jnp.where(row == 0, prev[SUBLANES - 2:SUBLANES - 1, :],
                   jnp.where(row == 1, prev[SUBLANES - 1:SUBLANES, :], pltpu.roll(u, 2, axis=0)))
    carry_ref[j] = u[tm - SUBLANES:tm, :]
    cw = cw_ref[...]
    conv = cw[0:1, :] * u2 + cw[1:2, :] * u1 + cw[2:3, :] * u
    sc_ref[...] = (seg(3) * conv).astype(BF16)


PROJ_TN = 256
PROJ_SEGMENTS = 6


def _proj_weights(w_in):
    n, d, _ = w_in.shape
    nj = ATT_WIDTH // PROJ_TN
    w = w_in.astype(BF16).reshape(n, d, PROJ_SEGMENTS, nj, PROJ_TN)
    return w.transpose(0, 1, 3, 2, 4).reshape(n, d, PROJ_SEGMENTS * ATT_WIDTH)


def _proj(h, g, w_in, conv_w, layer, bsz, seq, rope):
    d = D_MODEL
    tm = _pick(seq, 1024)
    tn = PROJ_TN
    nsi = seq // tm
    nj = ATT_WIDTH // tn
    tspec = pl.BlockSpec((tm, HEAD_DIM), lambda i, j: (i % nsi, 0))
    ospec = pl.BlockSpec((tm, tn), lambda i, j: (i, j))
    oshape = jax.ShapeDtypeStruct((bsz * seq, ATT_WIDTH), BF16)
    tspec_t = pl.BlockSpec((tn, tm), lambda i, j: (j, i))
    oshape_t = jax.ShapeDtypeStruct((ATT_WIDTH, bsz * seq), BF16)
    vmem = (2 * tm * d * 4 + tm * d * 2 + 2 * PROJ_SEGMENTS * d * tn * 2 + 2 * 4 * tm * tn * 2
            + 2 * 4 * tm * HEAD_DIM * 4 + 12 * tm * tn * 4 + 4 * 2**20)
    return pl.pallas_call(
        functools.partial(_proj_body, tiles_per_seq=nsi),
        grid=(bsz * nsi, nj),
        in_specs=[pl.BlockSpec((tm, d), lambda i, j: (i, 0)),
                  pl.BlockSpec((1, d), lambda i, j: (0, 0)),
                  pl.BlockSpec((None, d, PROJ_SEGMENTS * tn), lambda i, j: (layer, 0, j))]
                 + [tspec] * 4
                 + [pl.BlockSpec((None, CONV_K, tn), lambda i, j: (layer, 0, j))],
        out_specs=[tspec_t, ospec, tspec_t, ospec],
        out_shape=[oshape_t, oshape, oshape_t, oshape],
        scratch_shapes=[pltpu.VMEM((tm, d), BF16), pltpu.VMEM((nj, SUBLANES, tn), F32)],
        compiler_params=_params(("arbitrary", "arbitrary"), vmem),
        name="mix_proj",
    )(h, g.reshape(1, d), w_in, *rope, conv_w)


def _attn_bias(seq):
    c = np.arange(seq)[:, None]
    r = np.arange(ATT_Q_TILE)[None, :]
    delta = (seq - ATT_Q_TILE) + r - c
    count = np.zeros(delta.shape, np.int64)
    for window, dil in DILATED_PATTERN:
        count += (delta >= 0) & (delta % dil == 0) & (delta <= window)
    return np.where(count > 0, np.log2(np.maximum(count, 1)), NEG_INF).astype(np.float32)


def _attn_body(qt_ref, k_ref, vt_ref, bias_ref, o_ref):
    seq = k_ref.shape[0]
    nq = seq // ATT_Q_TILE

    def scores(i):
        nk = (i + 1) * ATT_Q_TILE
        return _dot(k_ref[0:nk, :], qt_ref[:, i * ATT_Q_TILE:nk]) + bias_ref[seq - nk:seq, :]

    s_next = scores(0)
    for i in range(nq):
        nk = (i + 1) * ATT_Q_TILE
        s = s_next
        if i + 1 < nq:
            s_next = scores(i + 1)
        m = jnp.max(s, axis=0, keepdims=True)
        p = jnp.exp2(s - m)
        l = jnp.sum(p, axis=0, keepdims=True)
        ot = _dot(vt_ref[:, 0:nk], p.astype(BF16)) / l
        o_ref[i * ATT_Q_TILE:nk, :] = ot.T.astype(o_ref.dtype)


def _attn(qt, k, vt, bsz, seq):
    bias = jnp.asarray(_attn_bias(seq))
    spec = pl.BlockSpec((seq, HEAD_DIM), lambda b, h: (b, h))
    spec_t = pl.BlockSpec((HEAD_DIM, seq), lambda b, h: (h, b))
    vmem = 2 * 4 * seq * HEAD_DIM * 2 + 2 * ATT_Q_TILE * seq * 4 + 6 * ATT_Q_TILE * seq * 4 + 4 * 2**20
    return pl.pallas_call(
        _attn_body,
        grid=(bsz, ATT_HEADS),
        in_specs=[spec_t, spec, spec_t, pl.BlockSpec((seq, ATT_Q_TILE), lambda b, h: (0, 0))],
        out_specs=spec,
        out_shape=jax.ShapeDtypeStruct((bsz * seq, ATT_WIDTH), BF16),
        compiler_params=_params(("arbitrary", "arbitrary"), vmem),
        name="dilated_attn",
    )(qt, k, vt, bias)


def _outproj_body(h_ref, a_ref, s_ref, wa_ref, ws_ref, o_ref):
    o_ref[...] = h_ref[...] + _dot(a_ref[...], wa_ref[...]) + _dot(s_ref[...], ws_ref[...])


def _outproj(h, att, sc, w_out, layer, bsz, seq):
    d = D_MODEL
    tm = _pick(bsz * seq, 512)
    hspec = pl.BlockSpec((tm, d), lambda i: (i, 0))
    aspec = pl.BlockSpec((tm, ATT_WIDTH), lambda i: (i, 0))
    wspec = lambda seg: pl.BlockSpec((None, ATT_WIDTH, d), lambda i: (layer, seg, 0),
                                     pipeline_mode=pl.Buffered(1))
    vmem = 4 * tm * d * 4 + 4 * tm * ATT_WIDTH * 2 + 2 * ATT_WIDTH * d * 2 + 3 * tm * d * 4 + 4 * 2**20
    return pl.pallas_call(
        _outproj_body,
        grid=(bsz * seq // tm,),
        in_specs=[hspec, aspec, aspec, wspec(0), wspec(1)],
        out_specs=hspec,
        out_shape=jax.ShapeDtypeStruct((bsz * seq, d), F32),
        compiler_params=_params(("arbitrary",), vmem),
        name="mix_out",
    )(h, att, sc, w_out, w_out)


def _disc_body(lr_ref, li_ref, ldt_ref, bre_ref, bim_ref, cre_ref, cim_ref,
               ar_ref, ai_ref, bb_ref, cb_ref):
    lr = lr_ref[0]
    li = li_ref[0]
    dt = jnp.exp(ldt_ref[0])
    mag = jnp.exp(lr * dt)
    ar = mag * jnp.cos(li * dt)
    ai = mag * jnp.sin(li * dt)
    den = lr * lr + li * li
    fr = ((ar - 1.0) * lr + ai * li) / den
    fi = (ai * lr - (ar - 1.0) * li) / den
    bre = bre_ref[0]
    bim = bim_ref[0]
    ar_ref[0] = ar
    ai_ref[0] = ai
    bb_ref[0, :, 0:SLAB_STATE] = (fr * bre - fi * bim).astype(BF16)
    bb_ref[0, :, SLAB_STATE:2 * SLAB_STATE] = (fr * bim + fi * bre).astype(BF16)
    cb_ref[0, 0:SLAB_STATE, :] = cre_ref[0].astype(BF16)
    cb_ref[0, SLAB_STATE:2 * SLAB_STATE, :] = (-cim_ref[0]).astype(BF16)


def _blockdiag(t):
    _, g, a, b = t.shape
    eye = jnp.eye(g, dtype=t.dtype)
    return jnp.einsum("sgab,gh->sgahb", t, eye).reshape(N_SLABS, g * a, g * b)


def _s5_discretise(lam_re, lam_im, log_dt, b_re, b_im, c_re, c_im):
    vec = lambda t: t.reshape(N_SLABS, 1, SLAB_STATE)
    ldt = jnp.repeat(log_dt, S5_STATE)
    bblk = lambda t: _blockdiag(
        t.reshape(N_SLABS, SLAB_GROUPS, S5_STATE, S5_GROUP).transpose(0, 1, 3, 2))
    cblk = lambda t: _blockdiag(
        t.reshape(N_SLABS, SLAB_GROUPS, S5_GROUP, S5_STATE).transpose(0, 1, 3, 2))
    vspec = pl.BlockSpec((1, 1, SLAB_STATE), lambda s: (s, 0, 0))
    bspec = pl.BlockSpec((1, LANES, SLAB_STATE), lambda s: (s, 0, 0))
    cspec = pl.BlockSpec((1, SLAB_STATE, LANES), lambda s: (s, 0, 0))
    return pl.pallas_call(
        _disc_body,
        grid=(N_SLABS,),
        in_specs=[vspec, vspec, vspec, bspec, bspec, cspec, cspec],
        out_specs=[vspec, vspec,
                   pl.BlockSpec((1, LANES, 2 * SLAB_STATE), lambda s: (s, 0, 0)),
                   pl.BlockSpec((1, 2 * SLAB_STATE, LANES), lambda s: (s, 0, 0))],
        out_shape=[jax.ShapeDtypeStruct((N_SLABS, 1, SLAB_STATE), F32)] * 2
                  + [jax.ShapeDtypeStruct((N_SLABS, LANES, 2 * SLAB_STATE), BF16),
                     jax.ShapeDtypeStruct((N_SLABS, 2 * SLAB_STATE, LANES), BF16)],
        compiler_params=_params(("arbitrary",), 16 * 2**20),
        name="s5_discretise",
    )(vec(lam_re), vec(lam_im), vec(ldt), bblk(b_re), bblk(b_im), cblk(c_re), cblk(c_im))


def _gelu_tanh(x):
    return x * (0.5 * (1.0 + jnp.tanh(math.sqrt(2.0 / math.pi) * (x + 0.044715 * (x * x * x)))))


def _s5_body(h_ref, g_ref, ar_ref, ai_ref, bb_ref, cb_ref, d_ref, y_ref,
             ut_ref, ys_ref, zs0_ref, zs1_ref, st_ref, *, steps, nb):
    @pl.when(pl.program_id(0) == 0)
    def _():
        st_ref[...] = jnp.zeros(st_ref.shape, F32)

    g = g_ref[...]
    for b in range(nb):
        un = _rms(h_ref[b], g)
        for s in range(N_SLABS):
            ut_ref[s, pl.ds(b, steps, stride=nb), :] = un[:, s * LANES:(s + 1) * LANES]

    def input_map(s, zs_ref):
        zs_ref[...] = _dot(ut_ref[s].astype(BF16), bb_ref[s])

    def recurrence(s, zs_ref):
        ar = jnp.broadcast_to(ar_ref[s], (nb, SLAB_STATE))
        ai = jnp.broadcast_to(ai_ref[s], (nb, SLAB_STATE))
        z_re = st_ref[s, :, 0:SLAB_STATE]
        z_im = st_ref[s, :, SLAB_STATE:2 * SLAB_STATE]
        for t in range(steps):
            r = slice(t * nb, (t + 1) * nb)
            z_re, z_im = (ar * z_re - ai * z_im + zs_ref[r, 0:SLAB_STATE],
                          ar * z_im + ai * z_re + zs_ref[r, SLAB_STATE:2 * SLAB_STATE])
            zs_ref[r, 0:SLAB_STATE] = z_re
            zs_ref[r, SLAB_STATE:2 * SLAB_STATE] = z_im
        st_ref[s, :, 0:SLAB_STATE] = z_re
        st_ref[s, :, SLAB_STATE:2 * SLAB_STATE] = z_im

    def output_map(s, zs_ref):
        y = _dot(zs_ref[...].astype(BF16), cb_ref[s]) + ut_ref[s] * d_ref[s]
        ys_ref[s] = _gelu_tanh(y)

    input_map(0, zs0_ref)
    input_map(1, zs1_ref)
    recurrence(0, zs0_ref)

    def pair(p, _):
        s = 2 * p
        output_map(s - 2, zs0_ref)
        input_map(s, zs0_ref)
        recurrence(s - 1, zs1_ref)
        output_map(s - 1, zs1_ref)
        input_map(s + 1, zs1_ref)
        recurrence(s, zs0_ref)
        return 0

    lax.fori_loop(1, N_SLABS // 2, pair, 0)
    output_map(N_SLABS - 2, zs0_ref)
    recurrence(N_SLABS - 1, zs1_ref)
    output_map(N_SLABS - 1, zs1_ref)

    for b in range(nb):
        for s in range(N_SLABS):
            y_ref[b, :, s * LANES:(s + 1) * LANES] = (
                ys_ref[s, pl.ds(b, steps, stride=nb), :].astype(BF16))


def _s5_scan(h, g, ar, ai, bb, cb, d_skip, bsz, seq):
    d = D_MODEL
    steps = _pick(seq, 32)
    rows = steps * bsz
    const = lambda shape: pl.BlockSpec(shape, lambda i: (0,) * len(shape),
                                       pipeline_mode=pl.Buffered(1))
    blk = pl.BlockSpec((bsz, steps, d), lambda i: (0, i, 0))
    vmem = (2 * rows * d * 4 + 2 * rows * d * 2 + 2 * N_SLABS * LANES * 2 * SLAB_STATE * 2
            + 2 * rows * d * 4 + 2 * rows * 2 * SLAB_STATE * 4 + bsz * d * 8 * 4
            + 2 * rows * d * 4 + 4 * 2**20)
    return pl.pallas_call(
        functools.partial(_s5_body, steps=steps, nb=bsz),
        grid=(seq // steps,),
        in_specs=[blk,
                  const((1, d)),
                  const((N_SLABS, 1, SLAB_STATE)), const((N_SLABS, 1, SLAB_STATE)),
                  const((N_SLABS, LANES, 2 * SLAB_STATE)), const((N_SLABS, 2 * SLAB_STATE, LANES)),
                  const((N_SLABS, 1, LANES))],
        out_specs=blk,
        out_shape=jax.ShapeDtypeStruct((bsz, seq, d), BF16),
        scratch_shapes=[pltpu.VMEM((N_SLABS, rows, LANES), F32),
                        pltpu.VMEM((N_SLABS, rows, LANES), F32),
                        pltpu.VMEM((rows, 2 * SLAB_STATE), F32),
                        pltpu.VMEM((rows, 2 * SLAB_STATE), F32),
                        pltpu.VMEM((N_SLABS, bsz, 2 * SLAB_STATE), F32)],
        compiler_params=_params(("arbitrary",), vmem),
        name="s5_scan",
    )(h, g.reshape(1, d), ar, ai, bb, cb, d_skip.reshape(N_SLABS, 1, LANES))


def _glu_body(h_ref, y_ref, wa_ref, wb_ref, o_ref):
    y = y_ref[...]
    o_ref[...] = h_ref[...] + _dot(y, wa_ref[...]) * jax.nn.sigmoid(_dot(y, wb_ref[...]))


def _glu(h, y, wa, wb, layer):
    n, d = h.shape
    tm = _pick(n, 512)
    hspec = pl.BlockSpec((tm, d), lambda i: (i, 0))
    wspec = pl.BlockSpec((None, d, d), lambda i: (layer, 0, 0), pipeline_mode=pl.Buffered(1))
    vmem = 4 * tm * d * 4 + 2 * tm * d * 2 + 2 * d * d * 2 + 4 * tm * d * 4 + 4 * 2**20
    return pl.pallas_call(
        _glu_body,
        grid=(n // tm,),
        in_specs=[hspec, hspec, wspec, wspec],
        out_specs=hspec,
        out_shape=jax.ShapeDtypeStruct((n, d), F32),
        compiler_params=_params(("arbitrary",), vmem),
        name="s5_glu",
    )(h, y, wa, wb)


def _rope_tables(seq):
    half = HEAD_DIM // 2
    inv = ROPE_THETA ** (-jnp.arange(0, half, dtype=F32) * 2.0 / HEAD_DIM)
    ang = jnp.arange(seq, dtype=F32)[:, None] * inv[None, :]
    cos, sin = jnp.cos(ang), jnp.sin(ang)
    c2 = jnp.concatenate([cos, cos], axis=-1)
    s2 = jnp.concatenate([-sin, sin], axis=-1)
    scale = HEAD_DIM ** -0.5 * math.log2(math.e)
    return c2 * scale, s2 * scale, c2, s2


def kernel(x, ln_ffn_pre, ln_mix, ln_ffn_post, ln_final, ffn_w1, ffn_w3, ffn_w2, ab_w_in, ab_conv_w, ab_w_out, s5_lambda_re, s5_lambda_im, s5_log_dt, s5_b_re, s5_b_im, s5_c_re, s5_c_im, s5_d, s5_glu_wa, s5_glu_wb):
    bsz, seq, d = x.shape
    depth = ln_mix.shape[0]
    assert d == D_MODEL and seq % ATT_Q_TILE == 0 and bsz % SUBLANES == 0
    w1, w3, w2 = ffn_w1.astype(BF16), ffn_w3.astype(BF16), ffn_w2.astype(BF16)
    w_in, w_out = _proj_weights(ab_w_in), ab_w_out.astype(BF16)
    wa, wb = s5_glu_wa.astype(BF16), s5_glu_wb.astype(BF16)
    rope = _rope_tables(seq)

    h = x.reshape(bsz * seq, d)
    for i in range(depth):
        j = i // 2
        last = i == depth - 1
        h = _ffn(h, ln_ffn_pre[i], w1, w3, w2, i, 0)
        if i % 2 == 0:
            qt, k, vt, sc = _proj(h, ln_mix[i], w_in, ab_conv_w, j, bsz, seq, rope)
            att = _attn(qt, k, vt, bsz, seq)
            h = _outproj(h, att, sc, w_out, j, bsz, seq)
        else:
            ar, ai, bb, cb = _s5_discretise(s5_lambda_re[j], s5_lambda_im[j], s5_log_dt[j],
                                            s5_b_re[j], s5_b_im[j], s5_c_re[j], s5_c_im[j])
            y = _s5_scan(h.reshape(bsz, seq, d), ln_mix[i], ar, ai, bb, cb, s5_d[j], bsz, seq)
            h = _glu(h, y.reshape(bsz * seq, d), wa, wb, j)
        h = _ffn(h, ln_ffn_post[i], w1, w3, w2, i, 1, g_final=ln_final if last else None)
    return h.reshape(bsz, seq, d)
```

```python
import functools
import math

import numpy as np
import jax
import jax.numpy as jnp
from jax import lax
from jax.experimental import pallas as pl
from jax.experimental.pallas import tpu as pltpu

F32 = jnp.float32
BF16 = jnp.bfloat16

D_MODEL = 2048
HEAD_DIM = 128
ATT_WIDTH = 1024
ATT_HEADS = ATT_WIDTH // HEAD_DIM
CONV_CHANNELS = 1024
CONV_K = 3
DILATED_PATTERN = ((128, 1), (512, 4), (2048, 16))
S5_GROUP = 16
S5_STATE = 64
S5_GROUPS = D_MODEL // S5_GROUP
ROPE_THETA = 10000.0
RMS_EPS = 1e-6
NEG_INF = -1e30

LANES = 128
SUBLANES = 8
VMEM_CAP_BYTES = 60 * 2**20
SLAB_GROUPS = LANES // S5_GROUP
N_SLABS = S5_GROUPS // SLAB_GROUPS
SLAB_STATE = SLAB_GROUPS * S5_STATE
ATT_Q_TILE = 512


def _params(semantics, vmem_bytes):
    return pltpu.CompilerParams(dimension_semantics=semantics,
                                vmem_limit_bytes=min(int(vmem_bytes), VMEM_CAP_BYTES))


def _pick(n, pref):
    t = min(n, pref)
    while n % t:
        t //= 2
    return t


def _rms(x, g):
    ms = jnp.mean(x * x, axis=-1, keepdims=True)
    return x * lax.rsqrt(ms + RMS_EPS) * g


def _dot(a, b):
    return jnp.dot(a, b, preferred_element_type=F32)


def _ffn_body(*refs, final_norm):
    if final_norm:
        h_ref, g_ref, w1_ref, w3_ref, w2_ref, gf_ref, o_ref, xn_ref = refs
    else:
        h_ref, g_ref, w1_ref, w3_ref, w2_ref, o_ref, xn_ref = refs
    j = pl.program_id(1)

    @pl.when(j == 0)
    def _():
        x = h_ref[...]
        xn_ref[...] = _rms(x, g_ref[...]).astype(BF16)
        o_ref[...] = x

    xn = xn_ref[...]
    a = _dot(xn, w1_ref[...])
    b = _dot(xn, w3_ref[...])
    act = (a * jax.nn.sigmoid(a)) * b * 0.5
    o_ref[...] += _dot(act.astype(BF16), w2_ref[...])

    if final_norm:
        @pl.when(j == pl.num_programs(1) - 1)
        def _():
            o_ref[...] = _rms(o_ref[...], gf_ref[...])


FFN_TM = 1024
FFN_TF = 512


def _ffn(h, g, w1, w3, w2, layer, half, g_final=None):
    n, d = h.shape
    f = w1.shape[-1]
    tm = _pick(n, FFN_TM)
    tf = _pick(f, FFN_TF)
    in_specs = [
        pl.BlockSpec((tm, d), lambda i, j: (i, 0)),
        pl.BlockSpec((1, d), lambda i, j: (0, 0)),
        pl.BlockSpec((None, None, d, tf), lambda i, j: (layer, half, 0, j)),
        pl.BlockSpec((None, None, d, tf), lambda i, j: (layer, half, 0, j)),
        pl.BlockSpec((None, None, tf, d), lambda i, j: (layer, half, j, 0)),
    ]
    args = [h, g.reshape(1, d), w1, w3, w2]
    if g_final is not None:
        in_specs.append(pl.BlockSpec((1, d), lambda i, j: (0, 0)))
        args.append(g_final.reshape(1, d))
    vmem = 4 * tm * d * 4 + tm * d * 2 + 2 * 3 * d * tf * 2 + 4 * tm * tf * 4 + 4 * 2**20
    return pl.pallas_call(
        functools.partial(_ffn_body, final_norm=g_final is not None),
        grid=(n // tm, f // tf),
        in_specs=in_specs,
        out_specs=pl.BlockSpec((tm, d), lambda i, j: (i, 0)),
        out_shape=jax.ShapeDtypeStruct((n, d), F32),
        scratch_shapes=[pltpu.VMEM((tm, d), BF16)],
        compiler_params=_params(("arbitrary", "arbitrary"), vmem),
        name="ffn",
    )(*args)


def _proj_body(h_ref, g_ref, w_ref, cq_ref, sq_ref, ck_ref, sk_ref, cw_ref,
               qt_ref, k_ref, vt_ref, sc_ref, xn_ref, carry_ref, *, tiles_per_seq):
    i = pl.program_id(0)
    j = pl.program_id(1)
    tm, tn = k_ref.shape

    @pl.when(j == 0)
    def _():
        xn_ref[...] = _rms(h_ref[...], g_ref[...]).astype(BF16)

    p = _dot(xn_ref[...], w_ref[...])
    seg = lambda n: p[:, n * tn:(n + 1) * tn]

    def rope(t, c, s):
        heads = []
        for hh in range(tn // HEAD_DIM):
            th = t[:, hh * HEAD_DIM:(hh + 1) * HEAD_DIM]
            heads.append(th * c + pltpu.roll(th, HEAD_DIM // 2, axis=1) * s)
        return jnp.concatenate(heads, axis=1)

    qt_ref[...] = rope(seg(0), cq_ref[...], sq_ref[...]).T.astype(BF16)
    k_ref[...] = rope(seg(1), ck_ref[...], sk_ref[...]).astype(BF16)
    vt_ref[...] = seg(2).T.astype(BF16)

    u = seg(4) * seg(5)

    @pl.when(i % tiles_per_seq == 0)
    def _():
        carry_ref[j] = jnp.zeros((SUBLANES, tn), F32)

    prev = carry_ref[j]
    row = lax.broadcasted_iota(jnp.int32, (tm, tn), 0)
    u1 = jnp.where(row == 0, prev[SUBLANES - 1:SUBLANES, :], pltpu.roll(u, 1, axis=0))
    u2 = jnp.where(row == 0, prev[SUBLANES - 2:SUBLANES - 1, :],
                   jnp.where(row == 1, prev[SUBLANES - 1:SUBLANES, :], pltpu.roll(u, 2, axis=0)))
    carry_ref[j] = u[tm - SUBLANES:tm, :]
    cw = cw_ref[...]
    conv = cw[0:1, :] * u2 + cw[1:2, :] * u1 + cw[2:3, :] * u
    sc_ref[...] = (seg(3) * conv).astype(BF16)


PROJ_TN = 256
PROJ_SEGMENTS = 6


def _proj_weights(w_in):
    n, d, _ = w_in.shape
    nj = ATT_WIDTH // PROJ_TN
    w = w_in.astype(BF16).reshape(n, d, PROJ_SEGMENTS, nj, PROJ_TN)
    return w.transpose(0, 1, 3, 2, 4).reshape(n, d, PROJ_SEGMENTS * ATT_WIDTH)


def _proj(h, g, w_in, conv_w, layer, bsz, seq, rope):
    d = D_MODEL
    tm = _pick(seq, 1024)
    tn = PROJ_TN
    nsi = seq // tm
    nj = ATT_WIDTH // tn
    tspec = pl.BlockSpec((tm, HEAD_DIM), lambda i, j: (i % nsi, 0))
    ospec = pl.BlockSpec((tm, tn), lambda i, j: (i, j))
    oshape = jax.ShapeDtypeStruct((bsz * seq, ATT_WIDTH), BF16)
    tspec_t = pl.BlockSpec((tn, tm), lambda i, j: (j, i))
    oshape_t = jax.ShapeDtypeStruct((ATT_WIDTH, bsz * seq), BF16)
    vmem = (2 * tm * d * 4 + tm * d * 2 + 2 * PROJ_SEGMENTS * d * tn * 2 + 2 * 4 * tm * tn * 2
            + 2 * 4 * tm * HEAD_DIM * 4 + 12 * tm * tn * 4 + 4 * 2**20)
    return pl.pallas_call(
        functools.partial(_proj_body, tiles_per_seq=nsi),
        grid=(bsz * nsi, nj),
        in_specs=[pl.BlockSpec((tm, d), lambda i, j: (i, 0)),
                  pl.BlockSpec((1, d), lambda i, j: (0, 0)),
                  pl.BlockSpec((None, d, PROJ_SEGMENTS * tn), lambda i, j: (layer, 0, j))]
                 + [tspec] * 4
                 + [pl.BlockSpec((None, CONV_K, tn), lambda i, j: (layer, 0, j))],
        out_specs=[tspec_t, ospec, tspec_t, ospec],
        out_shape=[oshape_t, oshape, oshape_t, oshape],
        scratch_shapes=[pltpu.VMEM((tm, d), BF16), pltpu.VMEM((nj, SUBLANES, tn), F32)],
        compiler_params=_params(("arbitrary", "arbitrary"), vmem),
        name="mix_proj",
    )(h, g.reshape(1, d), w_in, *rope, conv_w)


def _attn_bias(seq):
    c = np.arange(seq)[:, None]
    r = np.arange(ATT_Q_TILE)[None, :]
    delta = (seq - ATT_Q_TILE) + r - c
    count = np.zeros(delta.shape, np.int64)
    for window, dil in DILATED_PATTERN:
        count += (delta >= 0) & (delta % dil == 0) & (delta <= window)
    return np.where(count > 0, np.log2(np.maximum(count, 1)), NEG_INF).astype(np.float32)


def _attn_body(qt_ref, k_ref, vt_ref, bias_ref, o_ref):
    seq = k_ref.shape[0]
    nq = seq // ATT_Q_TILE

    def scores(i):
        nk = (i + 1) * ATT_Q_TILE
        return _dot(k_ref[0:nk, :], qt_ref[:, i * ATT_Q_TILE:nk]) + bias_ref[seq - nk:seq, :]

    s_next = scores(0)
    for i in range(nq):
        nk = (i + 1) * ATT_Q_TILE
        s = s_next
        if i + 1 < nq:
            s_next = scores(i + 1)
        m = jnp.max(s, axis=0, keepdims=True)
        p = jnp.exp2(s - m)
        l = jnp.sum(p, axis=0, keepdims=True)
        ot = _dot(vt_ref[:, 0:nk], p.astype(BF16)) / l
        o_ref[i * ATT_Q_TILE:nk, :] = ot.T.astype(o_ref.dtype)


def _attn(qt, k, vt, bsz, seq):
    bias = jnp.asarray(_attn_bias(seq))
    spec = pl.BlockSpec((seq, HEAD_DIM), lambda b, h: (b, h))
    spec_t = pl.BlockSpec((HEAD_DIM, seq), lambda b, h: (h, b))
    vmem = 2 * 4 * seq * HEAD_DIM * 2 + 2 * ATT_Q_TILE * seq * 4 + 6 * ATT_Q_TILE * seq * 4 + 4 * 2**20
    return pl.pallas_call(
        _attn_body,
        grid=(bsz, ATT_HEADS),
        in_specs=[spec_t, spec, spec_t, pl.BlockSpec((seq, ATT_Q_TILE), lambda b, h: (0, 0))],
        out_specs=spec,
        out_shape=jax.ShapeDtypeStruct((bsz * seq, ATT_WIDTH), BF16),
        compiler_params=_params(("arbitrary", "arbitrary"), vmem),
        name="dilated_attn",
    )(qt, k, vt, bias)


def _outproj_body(h_ref, a_ref, s_ref, wa_ref, ws_ref, o_ref):
    o_ref[...] = h_ref[...] + _dot(a_ref[...], wa_ref[...]) + _dot(s_ref[...], ws_ref[...])


def _outproj(h, att, sc, w_out, layer, bsz, seq):
    d = D_MODEL
    tm = _pick(bsz * seq, 512)
    hspec = pl.BlockSpec((tm, d), lambda i: (i, 0))
    aspec = pl.BlockSpec((tm, ATT_WIDTH), lambda i: (i, 0))
    wspec = lambda seg: pl.BlockSpec((None, ATT_WIDTH, d), lambda i: (layer, seg, 0),
                                     pipeline_mode=pl.Buffered(1))
    vmem = 4 * tm * d * 4 + 4 * tm * ATT_WIDTH * 2 + 2 * ATT_WIDTH * d * 2 + 3 * tm * d * 4 + 4 * 2**20
    return pl.pallas_call(
        _outproj_body,
        grid=(bsz * seq // tm,),
        in_specs=[hspec, aspec, aspec, wspec(0), wspec(1)],
        out_specs=hspec,
        out_shape=jax.ShapeDtypeStruct((bsz * seq, d), F32),
        compiler_params=_params(("arbitrary",), vmem),
        name="mix_out",
    )(h, att, sc, w_out, w_out)


def _disc_body(lr_ref, li_ref, ldt_ref, bre_ref, bim_ref, cre_ref, cim_ref,
               ar_ref, ai_ref, bb_ref, cb_ref):
    lr = lr_ref[0]
    li = li_ref[0]
    dt = jnp.exp(ldt_ref[0])
    mag = jnp.exp(lr * dt)
    ar = mag * jnp.cos(li * dt)
    ai = mag * jnp.sin(li * dt)
    den = lr * lr + li * li
    fr = ((ar - 1.0) * lr + ai * li) / den
    fi = (ai * lr - (ar - 1.0) * li) / den
    bre = bre_ref[0]
    bim = bim_ref[0]
    ar_ref[0] = ar
    ai_ref[0] = ai
    bb_ref[0, :, 0:SLAB_STATE] = (fr * bre - fi * bim).astype(BF16)
    bb_ref[0, :, SLAB_STATE:2 * SLAB_STATE] = (fr * bim + fi * bre).astype(BF16)
    cb_ref[0, 0:SLAB_STATE, :] = cre_ref[0].astype(BF16)
    cb_ref[0, SLAB_STATE:2 * SLAB_STATE, :] = (-cim_ref[0]).astype(BF16)


def _blockdiag(t):
    _, g, a, b = t.shape
    eye = jnp.eye(g, dtype=t.dtype)
    return jnp.einsum("sgab,gh->sgahb", t, eye).reshape(N_SLABS, g * a, g * b)


def _s5_discretise(lam_re, lam_im, log_dt, b_re, b_im, c_re, c_im):
    vec = lambda t: t.reshape(N_SLABS, 1, SLAB_STATE)
    ldt = jnp.repeat(log_dt, S5_STATE)
    bblk = lambda t: _blockdiag(
        t.reshape(N_SLABS, SLAB_GROUPS, S5_STATE, S5_GROUP).transpose(0, 1, 3, 2))
    cblk = lambda t: _blockdiag(
        t.reshape(N_SLABS, SLAB_GROUPS, S5_GROUP, S5_STATE).transpose(0, 1, 3, 2))
    vspec = pl.BlockSpec((1, 1, SLAB_STATE), lambda s: (s, 0, 0))
    bspec = pl.BlockSpec((1, LANES, SLAB_STATE), lambda s: (s, 0, 0))
    cspec = pl.BlockSpec((1, SLAB_STATE, LANES), lambda s: (s, 0, 0))
    return pl.pallas_call(
        _disc_body,
        grid=(N_SLABS,),
        in_specs=[vspec, vspec, vspec, bspec, bspec, cspec, cspec],
        out_specs=[vspec, vspec,
                   pl.BlockSpec((1, LANES, 2 * SLAB_STATE), lambda s: (s, 0, 0)),
                   pl.BlockSpec((1, 2 * SLAB_STATE, LANES), lambda s: (s, 0, 0))],
        out_shape=[jax.ShapeDtypeStruct((N_SLABS, 1, SLAB_STATE), F32)] * 2
                  + [jax.ShapeDtypeStruct((N_SLABS, LANES, 2 * SLAB_STATE), BF16),
                     jax.ShapeDtypeStruct((N_SLABS, 2 * SLAB_STATE, LANES), BF16)],
        compiler_params=_params(("arbitrary",), 16 * 2**20),
        name="s5_discretise",
    )(vec(lam_re), vec(lam_im), vec(ldt), bblk(b_re), bblk(b_im), cblk(c_re), cblk(c_im))


def _gelu_tanh(x):
    return x * (0.5 * (1.0 + jnp.tanh(math.sqrt(2.0 / math.pi) * (x + 0.044715 * (x * x * x)))))


def _s5_body(h_ref, g_ref, ar_ref, ai_ref, bb_ref, cb_ref, d_ref, y_ref,
             ut_ref, ys_ref, zs0_ref, zs1_ref, st_ref, *, steps, nb):
    @pl.when(pl.program_id(0) == 0)
    def _():
        st_ref[...] = jnp.zeros(st_ref.shape, F32)

    g = g_ref[...]
    for b in range(nb):
        un = _rms(h_ref[b], g)
        for s in range(N_SLABS):
            ut_ref[s, pl.ds(b, steps, stride=nb), :] = un[:, s * LANES:(s + 1) * LANES]

    def input_map(s, zs_ref):
        zs_ref[...] = _dot(ut_ref[s].astype(BF16), bb_ref[s])

    def recurrence(s, zs_ref):
        ar = jnp.broadcast_to(ar_ref[s], (nb, SLAB_STATE))
        ai = jnp.broadcast_to(ai_ref[s], (nb, SLAB_STATE))
        z_re = st_ref[s, :, 0:SLAB_STATE]
        z_im = st_ref[s, :, SLAB_STATE:2 * SLAB_STATE]
        for t in range(steps):
            r = slice(t * nb, (t + 1) * nb)
            z_re, z_im = (ar * z_re - ai * z_im + zs_ref[r, 0:SLAB_STATE],
                          ar * z_im + ai * z_re + zs_ref[r, SLAB_STATE:2 * SLAB_STATE])
            zs_ref[r, 0:SLAB_STATE] = z_re
            zs_ref[r, SLAB_STATE:2 * SLAB_STATE] = z_im
        st_ref[s, :, 0:SLAB_STATE] = z_re
        st_ref[s, :, SLAB_STATE:2 * SLAB_STATE] = z_im

    def output_map(s, zs_ref):
        y = _dot(zs_ref[...].astype(BF16), cb_ref[s]) + ut_ref[s] * d_ref[s]
        ys_ref[s] = _gelu_tanh(y)

    input_map(0, zs0_ref)
    input_map(1, zs1_ref)
    recurrence(0, zs0_ref)

    for s in range(2, N_SLABS, 2):
        output_map(s - 2, zs0_ref)
        input_map(s, zs0_ref)
        recurrence(s - 1, zs1_ref)
        output_map(s - 1, zs1_ref)
        input_map(s + 1, zs1_ref)
        recurrence(s, zs0_ref)
    output_map(N_SLABS - 2, zs0_ref)
    recurrence(N_SLABS - 1, zs1_ref)
    output_map(N_SLABS - 1, zs1_ref)

    for b in range(nb):
        for s in range(N_SLABS):
            y_ref[b, :, s * LANES:(s + 1) * LANES] = (
                ys_ref[s, pl.ds(b, steps, stride=nb), :].astype(BF16))


def _s5_scan(h, g, ar, ai, bb, cb, d_skip, bsz, seq):
    d = D_MODEL
    steps = _pick(seq, 32)
    rows = steps * bsz
    const = lambda shape: pl.BlockSpec(shape, lambda i: (0,) * len(shape),
                                       pipeline_mode=pl.Buffered(1))
    blk = pl.BlockSpec((bsz, steps, d), lambda i: (0, i, 0))
    vmem = (2 * rows * d * 4 + 2 * rows * d * 2 + 2 * N_SLABS * LANES * 2 * SLAB_STATE * 2
            + 2 * rows * d * 4 + 2 * rows * 2 * SLAB_STATE * 4 + bsz * d * 8 * 4
            + 2 * rows * d * 4 + 4 * 2**20)
    return pl.pallas_call(
        functools.partial(_s5_body, steps=steps, nb=bsz),
        grid=(seq // steps,),
        in_specs=[blk,
                  const((1, d)),
                  const((N_SLABS, 1, SLAB_STATE)), const((N_SLABS, 1, SLAB_STATE)),
                  const((N_SLABS, LANES, 2 * SLAB_STATE)), const((N_SLABS, 2 * SLAB_STATE, LANES)),
                  const((N_SLABS, 1, LANES))],
        out_specs=blk,
        out_shape=jax.ShapeDtypeStruct((bsz, seq, d), BF16),
        scratch_shapes=[pltpu.VMEM((N_SLABS, rows, LANES), F32),
                        pltpu.VMEM((N_SLABS, rows, LANES), F32),
                        pltpu.VMEM((rows, 2 * SLAB_STATE), F32),
                        pltpu.VMEM((rows, 2 * SLAB_STATE), F32),
                        pltpu.VMEM((N_SLABS, bsz, 2 * SLAB_STATE), F32)],
        compiler_params=_params(("arbitrary",), vmem),
        name="s5_scan",
    )(h, g.reshape(1, d), ar, ai, bb, cb, d_skip.reshape(N_SLABS, 1, LANES))


def _glu_body(h_ref, y_ref, wa_ref, wb_ref, o_ref):
    y = y_ref[...]
    o_ref[...] = h_ref[...] + _dot(y, wa_ref[...]) * jax.nn.sigmoid(_dot(y, wb_ref[...]))


def _glu(h, y, wa, wb, layer):
    n, d = h.shape
    tm = _pick(n, 512)
    hspec = pl.BlockSpec((tm, d), lambda i: (i, 0))
    wspec = pl.BlockSpec((None, d, d), lambda i: (layer, 0, 0), pipeline_mode=pl.Buffered(1))
    vmem = 4 * tm * d * 4 + 2 * tm * d * 2 + 2 * d * d * 2 + 4 * tm * d * 4 + 4 * 2**20
    return pl.pallas_call(
        _glu_body,
        grid=(n // tm,),
        in_specs=[hspec, hspec, wspec, wspec],
        out_specs=hspec,
        out_shape=jax.ShapeDtypeStruct((n, d), F32),
        compiler_params=_params(("arbitrary",), vmem),
        name="s5_glu",
    )(h, y, wa, wb)


def _rope_tables(seq):
    half = HEAD_DIM // 2
    inv = ROPE_THETA ** (-jnp.arange(0, half, dtype=F32) * 2.0 / HEAD_DIM)
    ang = jnp.arange(seq, dtype=F32)[:, None] * inv[None, :]
    cos, sin = jnp.cos(ang), jnp.sin(ang)
    c2 = jnp.concatenate([cos, cos], axis=-1)
    s2 = jnp.concatenate([-sin, sin], axis=-1)
    scale = HEAD_DIM ** -0.5 * math.log2(math.e)
    return c2 * scale, s2 * scale, c2, s2


def kernel(x, ln_ffn_pre, ln_mix, ln_ffn_post, ln_final, ffn_w1, ffn_w3, ffn_w2, ab_w_in, ab_conv_w, ab_w_out, s5_lambda_re, s5_lambda_im, s5_log_dt, s5_b_re, s5_b_im, s5_c_re, s5_c_im, s5_d, s5_glu_wa, s5_glu_wb):
    bsz, seq, d = x.shape
    depth = ln_mix.shape[0]
    assert d == D_MODEL and seq % ATT_Q_TILE == 0 and bsz % SUBLANES == 0
    w1, w3, w2 = ffn_w1.astype(BF16), ffn_w3.astype(BF16), ffn_w2.astype(BF16)
    w_in, w_out = _proj_weights(ab_w_in), ab_w_out.astype(BF16)
    wa, wb = s5_glu_wa.astype(BF16), s5_glu_wb.astype(BF16)
    rope = _rope_tables(seq)

    h = x.reshape(bsz * seq, d)
    for i in range(depth):
        j = i // 2
        last = i == depth - 1
        h = _ffn(h, ln_ffn_pre[i], w1, w3, w2, i, 0)
        if i % 2 == 0:
            qt, k, vt, sc = _proj(h, ln_mix[i], w_in, ab_conv_w, j, bsz, seq, rope)
            att = _attn(qt, k, vt, bsz, seq)
            h = _outproj(h, att, sc, w_out, j, bsz, seq)
        else:
            ar, ai, bb, cb = _s5_discretise(s5_lambda_re[j], s5_lambda_im[j], s5_log_dt[j],
                                            s5_b_re[j], s5_b_im[j], s5_c_re[j], s5_c_im[j])
            y = _s5_scan(h.reshape(bsz, seq, d), ln_mix[i], ar, ai, bb, cb, s5_d[j], bsz, seq)
            h = _glu(h, y.reshape(bsz * seq, d), wa, wb, j)
        h = _ffn(h, ln_ffn_post[i], w1, w3, w2, i, 1, g_final=ln_final if last else None)
    return h.reshape(bsz, seq, d)
```

```python
import functools
import math

import numpy as np
import jax
import jax.numpy as jnp
from jax import lax
from jax.experimental import pallas as pl
from jax.experimental.pallas import tpu as pltpu

F32 = jnp.float32
BF16 = jnp.bfloat16

D_MODEL = 2048
HEAD_DIM = 128
ATT_WIDTH = 1024
ATT_HEADS = ATT_WIDTH // HEAD_DIM
CONV_CHANNELS = 1024
CONV_K = 3
DILATED_PATTERN = ((128, 1), (512, 4), (2048, 16))
S5_GROUP = 16
S5_STATE = 64
S5_GROUPS = D_MODEL // S5_GROUP
ROPE_THETA = 10000.0
RMS_EPS = 1e-6
NEG_INF = -1e30

LANES = 128
SUBLANES = 8
VMEM_CAP_BYTES = 60 * 2**20
SLAB_GROUPS = LANES // S5_GROUP
N_SLABS = S5_GROUPS // SLAB_GROUPS
SLAB_STATE = SLAB_GROUPS * S5_STATE
ATT_Q_TILE = 512


def _params(semantics, vmem_bytes):
    return pltpu.CompilerParams(dimension_semantics=semantics,
                                vmem_limit_bytes=min(int(vmem_bytes), VMEM_CAP_BYTES))


def _pick(n, pref):
    t = min(n, pref)
    while n % t:
        t //= 2
    return t


def _rms(x, g):
    ms = jnp.mean(x * x, axis=-1, keepdims=True)
    return x * lax.rsqrt(ms + RMS_EPS) * g


def _dot(a, b):
    return jnp.dot(a, b, preferred_element_type=F32)


def _ffn_body(*refs, final_norm):
    if final_norm:
        h_ref, g_ref, w1_ref, w3_ref, w2_ref, gf_ref, o_ref, xn_ref = refs
    else:
        h_ref, g_ref, w1_ref, w3_ref, w2_ref, o_ref, xn_ref = refs
    j = pl.program_id(1)

    @pl.when(j == 0)
    def _():
        x = h_ref[...]
        xn_ref[...] = _rms(x, g_ref[...]).astype(BF16)
        o_ref[...] = x

    xn = xn_ref[...]
    a = _dot(xn, w1_ref[...])
    b = _dot(xn, w3_ref[...])
    act = (a * jax.nn.sigmoid(a)) * b * 0.5
    o_ref[...] += _dot(act.astype(BF16), w2_ref[...])

    if final_norm:
        @pl.when(j == pl.num_programs(1) - 1)
        def _():
            o_ref[...] = _rms(o_ref[...], gf_ref[...])


FFN_TM = 1024
FFN_TF = 512


def _ffn(h, g, w1, w3, w2, layer, half, g_final=None):
    n, d = h.shape
    f = w1.shape[-1]
    tm = _pick(n, FFN_TM)
    tf = _pick(f, FFN_TF)
    in_specs = [
        pl.BlockSpec((tm, d), lambda i, j: (i, 0)),
        pl.BlockSpec((1, d), lambda i, j: (0, 0)),
        pl.BlockSpec((None, None, d, tf), lambda i, j: (layer, half, 0, j)),
        pl.BlockSpec((None, None, d, tf), lambda i, j: (layer, half, 0, j)),
        pl.BlockSpec((None, None, tf, d), lambda i, j: (layer, half, j, 0)),
    ]
    args = [h, g.reshape(1, d), w1, w3, w2]
    if g_final is not None:
        in_specs.append(pl.BlockSpec((1, d), lambda i, j: (0, 0)))
        args.append(g_final.reshape(1, d))
    vmem = 4 * tm * d * 4 + tm * d * 2 + 2 * 3 * d * tf * 2 + 4 * tm * tf * 4 + 4 * 2**20
    return pl.pallas_call(
        functools.partial(_ffn_body, final_norm=g_final is not None),
        grid=(n // tm, f // tf),
        in_specs=in_specs,
        out_specs=pl.BlockSpec((tm, d), lambda i, j: (i, 0)),
        out_shape=jax.ShapeDtypeStruct((n, d), F32),
        scratch_shapes=[pltpu.VMEM((tm, d), BF16)],
        compiler_params=_params(("arbitrary", "arbitrary"), vmem),
        name="ffn",
    )(*args)


def _proj_body(h_ref, g_ref, w_ref, cq_ref, sq_ref, ck_ref, sk_ref, cw_ref,
               qt_ref, k_ref, vt_ref, sc_ref, xn_ref, carry_ref, *, tiles_per_seq):
    i = pl.program_id(0)
    j = pl.program_id(1)
    tm, tn = k_ref.shape

    @pl.when(j == 0)
    def _():
        xn_ref[...] = _rms(h_ref[...], g_ref[...]).astype(BF16)

    p = _dot(xn_ref[...], w_ref[...])
    seg = lambda n: p[:, n * tn:(n + 1) * tn]

    def rope(t, c, s):
        heads = []
        for hh in range(tn // HEAD_DIM):
            th = t[:, hh * HEAD_DIM:(hh + 1) * HEAD_DIM]
            heads.append(th * c + pltpu.roll(th, HEAD_DIM // 2, axis=1) * s)
        return jnp.concatenate(heads, axis=1)

    qt_ref[...] = rope(seg(0), cq_ref[...], sq_ref[...]).T.astype(BF16)
    k_ref[...] = rope(seg(1), ck_ref[...], sk_ref[...]).astype(BF16)
    vt_ref[...] = seg(2).T.astype(BF16)

    u = seg(4) * seg(5)

    @pl.when(i % tiles_per_seq == 0)
    def _():
        carry_ref[j] = jnp.zeros((SUBLANES, tn), F32)

    prev = carry_ref[j]
    row = lax.broadcasted_iota(jnp.int32, (tm, tn), 0)
    u1 = jnp.where(row == 0, prev[SUBLANES - 1:SUBLANES, :], pltpu.roll(u, 1, axis=0))
    u2 = jnp.where(row == 0, prev[SUBLANES - 2:SUBLANES - 1, :],
                   jnp.where(row == 1, prev[SUBLANES - 1:SUBLANES, :], pltpu.roll(u, 2, axis=0)))
    carry_ref[j] = u[tm - SUBLANES:tm, :]
    cw = cw_ref[...]
    conv = cw[0:1, :] * u2 + cw[1:2, :] * u1 + cw[2:3, :] * u
    sc_ref[...] = (seg(3) * conv).astype(BF16)


PROJ_TN = 256
PROJ_SEGMENTS = 6


def _proj_weights(w_in):
    n, d, _ = w_in.shape
    nj = ATT_WIDTH // PROJ_TN
    w = w_in.astype(BF16).reshape(n, d, PROJ_SEGMENTS, nj, PROJ_TN)
    return w.transpose(0, 1, 3, 2, 4).reshape(n, d, PROJ_SEGMENTS * ATT_WIDTH)


def _proj(h, g, w_in, conv_w, layer, bsz, seq, rope):
    d = D_MODEL
    tm = _pick(seq, 1024)
    tn = PROJ_TN
    nsi = seq // tm
    nj = ATT_WIDTH // tn
    tspec = pl.BlockSpec((tm, HEAD_DIM), lambda i, j: (i % nsi, 0))
    ospec = pl.BlockSpec((tm, tn), lambda i, j: (i, j))
    oshape = jax.ShapeDtypeStruct((bsz * seq, ATT_WIDTH), BF16)
    tspec_t = pl.BlockSpec((tn, tm), lambda i, j: (j, i))
    oshape_t = jax.ShapeDtypeStruct((ATT_WIDTH, bsz * seq), BF16)
    vmem = (2 * tm * d * 4 + tm * d * 2 + 2 * PROJ_SEGMENTS * d * tn * 2 + 2 * 4 * tm * tn * 2
            + 2 * 4 * tm * HEAD_DIM * 4 + 12 * tm * tn * 4 + 4 * 2**20)
    return pl.pallas_call(
        functools.partial(_proj_body, tiles_per_seq=nsi),
        grid=(bsz * nsi, nj),
        in_specs=[pl.BlockSpec((tm, d), lambda i, j: (i, 0)),
                  pl.BlockSpec((1, d), lambda i, j: (0, 0)),
                  pl.BlockSpec((None, d, PROJ_SEGMENTS * tn), lambda i, j: (layer, 0, j))]
                 + [tspec] * 4
                 + [pl.BlockSpec((None, CONV_K, tn), lambda i, j: (layer, 0, j))],
        out_specs=[tspec_t, ospec, tspec_t, ospec],
        out_shape=[oshape_t, oshape, oshape_t, oshape],
        scratch_shapes=[pltpu.VMEM((tm, d), BF16), pltpu.VMEM((nj, SUBLANES, tn), F32)],
        compiler_params=_params(("arbitrary", "arbitrary"), vmem),
        name="mix_proj",
    )(h, g.reshape(1, d), w_in, *rope, conv_w)


def _attn_bias(seq):
    c = np.arange(seq)[:, None]
    r = np.arange(ATT_Q_TILE)[None, :]
    delta = (seq - ATT_Q_TILE) + r - c
    count = np.zeros(delta.shape, np.int64)
    for window, dil in DILATED_PATTERN:
        count += (delta >= 0) & (delta % dil == 0) & (delta <= window)
    return np.where(count > 0, np.log2(np.maximum(count, 1)), NEG_INF).astype(np.float32)


ATT_HEADS_PER_STEP = 2


def _attn_body(qt_ref, k_ref, vt_ref, bias_ref, o_ref):
    seq = k_ref.shape[0]
    nq = seq // ATT_Q_TILE
    heads = [slice(hh * HEAD_DIM, (hh + 1) * HEAD_DIM) for hh in range(ATT_HEADS_PER_STEP)]

    def scores(i):
        nk = (i + 1) * ATT_Q_TILE
        return [_dot(k_ref[0:nk, hd], qt_ref[hd, i * ATT_Q_TILE:nk]) + bias_ref[seq - nk:seq, :]
                for hd in heads]

    s_next = scores(0)
    for i in range(nq):
        nk = (i + 1) * ATT_Q_TILE
        s_now = s_next
        if i + 1 < nq:
            s_next = scores(i + 1)
        for hd, s in zip(heads, s_now):
            m = jnp.max(s, axis=0, keepdims=True)
            p = jnp.exp2(s - m)
            l = jnp.sum(p, axis=0, keepdims=True)
            ot = _dot(vt_ref[hd, 0:nk], p.astype(BF16)) / l
            o_ref[i * ATT_Q_TILE:nk, hd] = ot.T.astype(o_ref.dtype)


def _attn(qt, k, vt, bsz, seq):
    bias = jnp.asarray(_attn_bias(seq))
    width = ATT_HEADS_PER_STEP * HEAD_DIM
    spec = pl.BlockSpec((seq, width), lambda b, h: (b, h))
    spec_t = pl.BlockSpec((width, seq), lambda b, h: (h, b))
    vmem = (2 * 4 * seq * width * 2 + 2 * ATT_Q_TILE * seq * 4
            + 6 * ATT_HEADS_PER_STEP * ATT_Q_TILE * seq * 4 + 4 * 2**20)
    return pl.pallas_call(
        _attn_body,
        grid=(bsz, ATT_HEADS // ATT_HEADS_PER_STEP),
        in_specs=[spec_t, spec, spec_t, pl.BlockSpec((seq, ATT_Q_TILE), lambda b, h: (0, 0))],
        out_specs=spec,
        out_shape=jax.ShapeDtypeStruct((bsz * seq, ATT_WIDTH), BF16),
        compiler_params=_params(("arbitrary", "arbitrary"), vmem),
        name="dilated_attn",
    )(qt, k, vt, bias)


def _outproj_body(h_ref, a_ref, s_ref, wa_ref, ws_ref, o_ref):
    o_ref[...] = h_ref[...] + _dot(a_ref[...], wa_ref[...]) + _dot(s_ref[...], ws_ref[...])


def _outproj(h, att, sc, w_out, layer, bsz, seq):
    d = D_MODEL
    tm = _pick(bsz * seq, 512)
    hspec = pl.BlockSpec((tm, d), lambda i: (i, 0))
    aspec = pl.BlockSpec((tm, ATT_WIDTH), lambda i: (i, 0))
    wspec = lambda seg: pl.BlockSpec((None, ATT_WIDTH, d), lambda i: (layer, seg, 0),
                                     pipeline_mode=pl.Buffered(1))
    vmem = 4 * tm * d * 4 + 4 * tm * ATT_WIDTH * 2 + 2 * ATT_WIDTH * d * 2 + 3 * tm * d * 4 + 4 * 2**20
    return pl.pallas_call(
        _outproj_body,
        grid=(bsz * seq // tm,),
        in_specs=[hspec, aspec, aspec, wspec(0), wspec(1)],
        out_specs=hspec,
        out_shape=jax.ShapeDtypeStruct((bsz * seq, d), F32),
        compiler_params=_params(("arbitrary",), vmem),
        name="mix_out",
    )(h, att, sc, w_out, w_out)


def _disc_body(lr_ref, li_ref, ldt_ref, bre_ref, bim_ref, cre_ref, cim_ref,
               ar_ref, ai_ref, bb_ref, cb_ref):
    lr = lr_ref[0]
    li = li_ref[0]
    dt = jnp.exp(ldt_ref[0])
    mag = jnp.exp(lr * dt)
    ar = mag * jnp.cos(li * dt)
    ai = mag * jnp.sin(li * dt)
    den = lr * lr + li * li
    fr = ((ar - 1.0) * lr + ai * li) / den
    fi = (ai * lr - (ar - 1.0) * li) / den
    bre = bre_ref[0]
    bim = bim_ref[0]
    ar_ref[0] = ar
    ai_ref[0] = ai
    bb_ref[0, :, 0:SLAB_STATE] = (fr * bre - fi * bim).astype(BF16)
    bb_ref[0, :, SLAB_STATE:2 * SLAB_STATE] = (fr * bim + fi * bre).astype(BF16)
    cb_ref[0, 0:SLAB_STATE, :] = cre_ref[0].astype(BF16)
    cb_ref[0, SLAB_STATE:2 * SLAB_STATE, :] = (-cim_ref[0]).astype(BF16)


def _blockdiag(t):
    _, g, a, b = t.shape
    eye = jnp.eye(g, dtype=t.dtype)
    return jnp.einsum("sgab,gh->sgahb", t, eye).reshape(N_SLABS, g * a, g * b)


def _s5_discretise(lam_re, lam_im, log_dt, b_re, b_im, c_re, c_im):
    vec = lambda t: t.reshape(N_SLABS, 1, SLAB_STATE)
    ldt = jnp.repeat(log_dt, S5_STATE)
    bblk = lambda t: _blockdiag(
        t.reshape(N_SLABS, SLAB_GROUPS, S5_STATE, S5_GROUP).transpose(0, 1, 3, 2))
    cblk = lambda t: _blockdiag(
        t.reshape(N_SLABS, SLAB_GROUPS, S5_GROUP, S5_STATE).transpose(0, 1, 3, 2))
    vspec = pl.BlockSpec((1, 1, SLAB_STATE), lambda s: (s, 0, 0))
    bspec = pl.BlockSpec((1, LANES, SLAB_STATE), lambda s: (s, 0, 0))
    cspec = pl.BlockSpec((1, SLAB_STATE, LANES), lambda s: (s, 0, 0))
    return pl.pallas_call(
        _disc_body,
        grid=(N_SLABS,),
        in_specs=[vspec, vspec, vspec, bspec, bspec, cspec, cspec],
        out_specs=[vspec, vspec,
                   pl.BlockSpec((1, LANES, 2 * SLAB_STATE), lambda s: (s, 0, 0)),
                   pl.BlockSpec((1, 2 * SLAB_STATE, LANES), lambda s: (s, 0, 0))],
        out_shape=[jax.ShapeDtypeStruct((N_SLABS, 1, SLAB_STATE), F32)] * 2
                  + [jax.ShapeDtypeStruct((N_SLABS, LANES, 2 * SLAB_STATE), BF16),
                     jax.ShapeDtypeStruct((N_SLABS, 2 * SLAB_STATE, LANES), BF16)],
        compiler_params=_params(("arbitrary",), 16 * 2**20),
        name="s5_discretise",
    )(vec(lam_re), vec(lam_im), vec(ldt), bblk(b_re), bblk(b_im), cblk(c_re), cblk(c_im))


def _gelu_tanh(x):
    return x * (0.5 * (1.0 + jnp.tanh(math.sqrt(2.0 / math.pi) * (x + 0.044715 * (x * x * x)))))


def _s5_body(h_ref, g_ref, ar_ref, ai_ref, bb_ref, cb_ref, d_ref, y_ref,
             ut_ref, ys_ref, zs0_ref, zs1_ref, st_ref, *, steps, nb):
    @pl.when(pl.program_id(0) == 0)
    def _():
        st_ref[...] = jnp.zeros(st_ref.shape, F32)

    g = g_ref[...]
    for b in range(nb):
        un = _rms(h_ref[b], g)
        for s in range(N_SLABS):
            ut_ref[s, pl.ds(b, steps, stride=nb), :] = un[:, s * LANES:(s + 1) * LANES]

    def input_map(s, zs_ref):
        zs_ref[...] = _dot(ut_ref[s].astype(BF16), bb_ref[s])

    def recurrence(s, zs_ref):
        ar = jnp.broadcast_to(ar_ref[s], (nb, SLAB_STATE))
        ai = jnp.broadcast_to(ai_ref[s], (nb, SLAB_STATE))
        z_re = st_ref[s, :, 0:SLAB_STATE]
        z_im = st_ref[s, :, SLAB_STATE:2 * SLAB_STATE]
        for t in range(steps):
            r = slice(t * nb, (t + 1) * nb)
            z_re, z_im = (ar * z_re - ai * z_im + zs_ref[r, 0:SLAB_STATE],
                          ar * z_im + ai * z_re + zs_ref[r, SLAB_STATE:2 * SLAB_STATE])
            zs_ref[r, 0:SLAB_STATE] = z_re
            zs_ref[r, SLAB_STATE:2 * SLAB_STATE] = z_im
        st_ref[s, :, 0:SLAB_STATE] = z_re
        st_ref[s, :, SLAB_STATE:2 * SLAB_STATE] = z_im

    def output_map(s, zs_ref):
        y = _dot(zs_ref[...].astype(BF16), cb_ref[s]) + ut_ref[s] * d_ref[s]
        ys_ref[s] = _gelu_tanh(y)

    input_map(0, zs0_ref)
    input_map(1, zs1_ref)
    recurrence(0, zs0_ref)

    for s in range(2, N_SLABS, 2):
        output_map(s - 2, zs0_ref)
        input_map(s, zs0_ref)
        recurrence(s - 1, zs1_ref)
        output_map(s - 1, zs1_ref)
        input_map(s + 1, zs1_ref)
        recurrence(s, zs0_ref)
    output_map(N_SLABS - 2, zs0_ref)
    recurrence(N_SLABS - 1, zs1_ref)
    output_map(N_SLABS - 1, zs1_ref)

    for b in range(nb):
        for s in range(N_SLABS):
            y_ref[b, :, s * LANES:(s + 1) * LANES] = (
                ys_ref[s, pl.ds(b, steps, stride=nb), :].astype(BF16))


def _s5_scan(h, g, ar, ai, bb, cb, d_skip, bsz, seq):
    d = D_MODEL
    steps = _pick(seq, 32)
    rows = steps * bsz
    const = lambda shape: pl.BlockSpec(shape, lambda i: (0,) * len(shape),
                                       pipeline_mode=pl.Buffered(1))
    blk = pl.BlockSpec((bsz, steps, d), lambda i: (0, i, 0))
    vmem = (2 * rows * d * 4 + 2 * rows * d * 2 + 2 * N_SLABS * LANES * 2 * SLAB_STATE * 2
            + 2 * rows * d * 4 + 2 * rows * 2 * SLAB_STATE * 4 + bsz * d * 8 * 4
            + 2 * rows * d * 4 + 4 * 2**20)
    return pl.pallas_call(
        functools.partial(_s5_body, steps=steps, nb=bsz),
        grid=(seq // steps,),
        in_specs=[blk,
                  const((1, d)),
                  const((N_SLABS, 1, SLAB_STATE)), const((N_SLABS, 1, SLAB_STATE)),
                  const((N_SLABS, LANES, 2 * SLAB_STATE)), const((N_SLABS, 2 * SLAB_STATE, LANES)),
                  const((N_SLABS, 1, LANES))],
        out_specs=blk,
        out_shape=jax.ShapeDtypeStruct((bsz, seq, d), BF16),
        scratch_shapes=[pltpu.VMEM((N_SLABS, rows, LANES), F32),
                        pltpu.VMEM((N_SLABS, rows, LANES), F32),
                        pltpu.VMEM((rows, 2 * SLAB_STATE), F32),
                        pltpu.VMEM((rows, 2 * SLAB_STATE), F32),
                        pltpu.VMEM((N_SLABS, bsz, 2 * SLAB_STATE), F32)],
        compiler_params=_params(("arbitrary",), vmem),
        name="s5_scan",
    )(h, g.reshape(1, d), ar, ai, bb, cb, d_skip.reshape(N_SLABS, 1, LANES))


def _glu_body(h_ref, y_ref, wa_ref, wb_ref, o_ref):
    y = y_ref[...]
    o_ref[...] = h_ref[...] + _dot(y, wa_ref[...]) * jax.nn.sigmoid(_dot(y, wb_ref[...]))


def _glu(h, y, wa, wb, layer):
    n, d = h.shape
    tm = _pick(n, 512)
    hspec = pl.BlockSpec((tm, d), lambda i: (i, 0))
    wspec = pl.BlockSpec((None, d, d), lambda i: (layer, 0, 0), pipeline_mode=pl.Buffered(1))
    vmem = 4 * tm * d * 4 + 2 * tm * d * 2 + 2 * d * d * 2 + 4 * tm * d * 4 + 4 * 2**20
    return pl.pallas_call(
        _glu_body,
        grid=(n // tm,),
        in_specs=[hspec, hspec, wspec, wspec],
        out_specs=hspec,
        out_shape=jax.ShapeDtypeStruct((n, d), F32),
        compiler_params=_params(("arbitrary",), vmem),
        name="s5_glu",
    )(h, y, wa, wb)


def _rope_tables(seq):
    half = HEAD_DIM // 2
    inv = ROPE_THETA ** (-jnp.arange(0, half, dtype=F32) * 2.0 / HEAD_DIM)
    ang = jnp.arange(seq, dtype=F32)[:, None] * inv[None, :]
    cos, sin = jnp.cos(ang), jnp.sin(ang)
    c2 = jnp.concatenate([cos, cos], axis=-1)
    s2 = jnp.concatenate([-sin, sin], axis=-1)
    scale = HEAD_DIM ** -0.5 * math.log2(math.e)
    return c2 * scale, s2 * scale, c2, s2


def kernel(x, ln_ffn_pre, ln_mix, ln_ffn_post, ln_final, ffn_w1, ffn_w3, ffn_w2, ab_w_in, ab_conv_w, ab_w_out, s5_lambda_re, s5_lambda_im, s5_log_dt, s5_b_re, s5_b_im, s5_c_re, s5_c_im, s5_d, s5_glu_wa, s5_glu_wb):
    bsz, seq, d = x.shape
    depth = ln_mix.shape[0]
    assert d == D_MODEL and seq % ATT_Q_TILE == 0 and bsz % SUBLANES == 0
    w1, w3, w2 = ffn_w1.astype(BF16), ffn_w3.astype(BF16), ffn_w2.astype(BF16)
    w_in, w_out = _proj_weights(ab_w_in), ab_w_out.astype(BF16)
    wa, wb = s5_glu_wa.astype(BF16), s5_glu_wb.astype(BF16)
    rope = _rope_tables(seq)

    h = x.reshape(bsz * seq, d)
    for i in range(depth):
        j = i // 2
        last = i == depth - 1
        h = _ffn(h, ln_ffn_pre[i], w1, w3, w2, i, 0)
        if i % 2 == 0:
            qt, k, vt, sc = _proj(h, ln_mix[i], w_in, ab_conv_w, j, bsz, seq, rope)
            att = _attn(qt, k, vt, bsz, seq)
            h = _outproj(h, att, sc, w_out, j, bsz, seq)
        else:
            ar, ai, bb, cb = _s5_discretise(s5_lambda_re[j], s5_lambda_im[j], s5_log_dt[j],
                                            s5_b_re[j], s5_b_im[j], s5_c_re[j], s5_c_im[j])
            y = _s5_scan(h.reshape(bsz, seq, d), ln_mix[i], ar, ai, bb, cb, s5_d[j], bsz, seq)
            h = _glu(h, y.reshape(bsz * seq, d), wa, wb, j)
        h = _ffn(h, ln_ffn_post[i], w1, w3, w2, i, 1, g_final=ln_final if last else None)
    return h.reshape(bsz, seq, d)
```
